```python
import jax, jax.numpy as jnp
from jax import lax
import numpy as np

D_MODEL = 1024
BATCH = 2
SEQ = 8192
DEPTH = 1
DEC_BATCH = 128
DEC_SEQ = 1
PAST_LEN = 8192
PAGE_SIZE = 128

N_HEADS = 8
HEAD_DIM = D_MODEL // N_HEADS
ATTN_W = N_HEADS * HEAD_DIM
MOBA_BLOCK = 256
MOBA_TOPK = 3
Q_BLOCK = 128
ROPE_THETA = 10000.0
SGU_GROUPS = 8
SGU_GROUP_DIM = D_MODEL // SGU_GROUPS
SGU_W = SGU_GROUPS * SGU_GROUP_DIM
SGU_CHUNK = 128
N_EXPERTS = 32
TOP_K = 4
D_FF = D_MODEL
SWIGLU_LIMIT = 7.0
SWIGLU_ALPHA = 1.702
MOE_BLOCK = 128
EPS = 1e-6
NEG = -1e30
PROJ_W = 3 * ATTN_W + 2 * SGU_W + 2 * D_MODEL
PROJ_SPLITS = (ATTN_W, 2 * ATTN_W, 3 * ATTN_W, 3 * ATTN_W + SGU_W,
               3 * ATTN_W + 2 * SGU_W, 3 * ATTN_W + 2 * SGU_W + D_MODEL)

kernel_name = "moba_gmlp_gated_hybrid_moe_step"


def _rms(x, g):
    xf = x.astype(jnp.float32)
    y = xf * lax.rsqrt(jnp.mean(xf * xf, axis=-1, keepdims=True) + EPS)
    return (y * g.astype(jnp.float32)).astype(x.dtype)


def _layernorm(x, g, b):
    xf = x.astype(jnp.float32)
    mu = jnp.mean(xf, axis=-1, keepdims=True)
    var = jnp.mean(jnp.square(xf - mu), axis=-1, keepdims=True)
    y = (xf - mu) * lax.rsqrt(var + EPS)
    return (y * g.astype(jnp.float32) + b.astype(jnp.float32)).astype(x.dtype)


def _rope(x, pos):
    inv = ROPE_THETA ** (-jnp.arange(0, HEAD_DIM, 2, dtype=jnp.float32) / HEAD_DIM)
    ang = pos.astype(jnp.float32)[:, None] * inv[None, :]
    cos = jnp.cos(ang)[:, None, :]
    sin = jnp.sin(ang)[:, None, :]
    xf = x.astype(jnp.float32)
    x1, x2 = xf[..., :HEAD_DIM // 2], xf[..., HEAD_DIM // 2:]
    return jnp.concatenate([x1 * cos - x2 * sin, x2 * cos + x1 * sin], axis=-1).astype(x.dtype)


def _attend_parts(scores, values):
    sizes = [s.shape[-1] for s in scores]
    p = jax.nn.softmax(jnp.concatenate(scores, axis=-1), axis=-1)
    out = 0.0
    off = 0
    for sz, (spec, val) in zip(sizes, values):
        out = out + jnp.einsum(spec, p[..., off:off + sz].astype(val.dtype), val)
        off += sz
    return out


def _moba_prompt(q, k, v):
    b, s = q.shape[0], q.shape[1]
    nb_full = s // MOBA_BLOCK
    n_blocks = -(-s // MOBA_BLOCK)
    pad = n_blocks * MOBA_BLOCK - s
    n_sel = min(MOBA_TOPK, nb_full)
    scale = HEAD_DIM ** -0.5
    qt, kt, vt = (a.transpose(0, 2, 1, 3) for a in (q, k, v))
    kp = jnp.pad(kt, ((0, 0), (0, 0), (0, pad), (0, 0)))
    vp = jnp.pad(vt, ((0, 0), (0, 0), (0, pad), (0, 0)))
    bi = jnp.arange(b)[:, None, None]
    hi = jnp.arange(N_HEADS)[None, :, None]
    if n_sel > 0:
        kb = kt[:, :, :nb_full * MOBA_BLOCK].reshape(b, N_HEADS, nb_full, MOBA_BLOCK, HEAD_DIM)
        vb = vt[:, :, :nb_full * MOBA_BLOCK].reshape(b, N_HEADS, nb_full, MOBA_BLOCK, HEAD_DIM)
        kmean = jnp.mean(kb.astype(jnp.float32), axis=3)

    def one_block(qc):
        start = qc * Q_BLOCK
        c = start // MOBA_BLOCK
        qq = lax.dynamic_slice_in_dim(qt, start, Q_BLOCK, axis=2)
        qpos = start + jnp.arange(Q_BLOCK)
        scores, values = [], []
        if n_sel > 0:
            bsc = jnp.einsum('bhqd,bhnd->bhqn', qq.astype(jnp.float32), kmean)
            bsc = jnp.where(jnp.arange(nb_full) < c, bsc, NEG)
            _, sel = lax.top_k(bsc, n_sel)
            for j in range(n_sel):
                idx = sel[..., j]
                k_sel = kb[bi, hi, idx]
                sc = jnp.einsum('bhqd,bhqkd->bhqk', qq, k_sel).astype(jnp.float32) * scale
                scores.append(jnp.where((idx < c)[..., None], sc, NEG))
                values.append(('bhqk,bhqkd->bhqd', vb[bi, hi, idx]))
        k_own = lax.dynamic_slice_in_dim(kp, c * MOBA_BLOCK, MOBA_BLOCK, axis=2)
        v_own = lax.dynamic_slice_in_dim(vp, c * MOBA_BLOCK, MOBA_BLOCK, axis=2)
        kpos = c * MOBA_BLOCK + jnp.arange(MOBA_BLOCK)
        so = jnp.einsum('bhqd,bhkd->bhqk', qq, k_own).astype(jnp.float32) * scale
        scores.append(jnp.where(kpos[None, :] <= qpos[:, None], so, NEG))
        values.append(('bhqk,bhkd->bhqd', v_own))
        return _attend_parts(scores, values)

    out = lax.map(one_block, jnp.arange(s // Q_BLOCK))
    return out.transpose(1, 0, 3, 2, 4).reshape(b, s, ATTN_W)


def _moba_sample(q, k, v, cache_k, cache_v, page_ksum, page_table, layer):
    n, l = q.shape[0], q.shape[1]
    past = page_table.shape[1] * PAGE_SIZE
    ppb = MOBA_BLOCK // PAGE_SIZE
    nb = past // MOBA_BLOCK
    n_sel = min(MOBA_TOPK, nb)
    rem_pages = (past % MOBA_BLOCK) // PAGE_SIZE
    scale = HEAD_DIM ** -0.5
    qt = q.transpose(0, 2, 1, 3)
    scores, values = [], []
    if n_sel > 0:
        blk = page_ksum[page_table[:, :nb * ppb]]
        kmean = blk.reshape(n, nb, ppb, N_HEADS, HEAD_DIM).sum(2) / MOBA_BLOCK
        bsc = jnp.einsum('nhqd,nbhd->nhqb', qt.astype(jnp.float32), kmean)
        _, sel = lax.top_k(bsc, n_sel)
        logical = sel[..., None] * ppb + jnp.arange(ppb)
        phys = page_table[jnp.arange(n)[:, None, None, None, None], logical]
        hi = jnp.arange(N_HEADS)[None, :, None, None, None, None]
        tok = jnp.arange(PAGE_SIZE)
        k_sel = cache_k[layer, phys[..., None], tok, hi].reshape(n, N_HEADS, l, n_sel * MOBA_BLOCK, HEAD_DIM)
        v_sel = cache_v[layer, phys[..., None], tok, hi].reshape(n, N_HEADS, l, n_sel * MOBA_BLOCK, HEAD_DIM)
        scores.append(jnp.einsum('nhqd,nhqkd->nhqk', qt, k_sel).astype(jnp.float32) * scale)
        values.append(('nhqk,nhqkd->nhqd', v_sel))
    if rem_pages > 0:
        own = page_table[:, nb * ppb: nb * ppb + rem_pages]
        k_own = cache_k[layer, own].reshape(n, rem_pages * PAGE_SIZE, N_HEADS, HEAD_DIM)
        v_own = cache_v[layer, own].reshape(n, rem_pages * PAGE_SIZE, N_HEADS, HEAD_DIM)
        scores.append(jnp.einsum('nhqd,nkhd->nhqk', qt, k_own).astype(jnp.float32) * scale)
        values.append(('nhqk,nkhd->nhqd', v_own))
    causal = jnp.tril(jnp.ones((l, l), dtype=bool))
    sn = jnp.einsum('nhqd,nkhd->nhqk', qt, k).astype(jnp.float32) * scale
    scores.append(jnp.where(causal, sn, NEG))
    values.append(('nhqk,nkhd->nhqd', v))
    out = _attend_parts(scores, values)
    return out.transpose(0, 2, 1, 3).reshape(n, l, ATTN_W)


def _spatial(vr, w_s, b_s):
    l = vr.shape[-3]
    wm = (w_s * jnp.tril(jnp.ones((SGU_CHUNK, SGU_CHUNK), w_s.dtype)))[:, :l, :l]
    return jnp.einsum('gts,...sgc->...tgc', wm, vr) + b_s[:, :l].T[:, :, None]


def _moe(h, w_router, b_router, w_gu, b_gu, w_dn, b_dn):
    t = h.shape[0]
    logits = (h @ w_router).astype(jnp.float32) + b_router.astype(jnp.float32)
    top_val, top_idx = lax.top_k(logits, TOP_K)
    gate = jax.nn.softmax(top_val, axis=-1)
    m = t * TOP_K
    e_flat = top_idx.reshape(-1)
    tok_flat = jnp.arange(m) // TOP_K
    order = jnp.argsort(e_flat)
    e_s, tok_s, g_s = e_flat[order], tok_flat[order], gate.reshape(-1)[order]
    counts = jnp.bincount(e_flat, length=N_EXPERTS)
    starts = jnp.cumsum(counts) - counts
    padded = (counts + MOE_BLOCK - 1) // MOE_BLOCK * MOE_BLOCK
    ends = jnp.cumsum(padded)
    pstarts = ends - padded
    dest = pstarts[e_s] + (jnp.arange(m) - starts[e_s])
    nblk = -(-(m + N_EXPERTS * (MOE_BLOCK - 1)) // MOE_BLOCK)
    buf_tok = jnp.zeros((nblk * MOE_BLOCK,), jnp.int32).at[dest].set(tok_s.astype(jnp.int32))
    blk_exp = jnp.clip(jnp.searchsorted(ends, jnp.arange(nblk) * MOE_BLOCK, side='right'), 0, N_EXPERTS - 1)

    def expert_block(args):
        ids, e = args
        xb = h[ids]
        gu = xb @ w_gu[e] + b_gu[e]
        g, u = gu[:, :D_FF], gu[:, D_FF:]
        g = jnp.minimum(g, SWIGLU_LIMIT)
        u = jnp.clip(u, -SWIGLU_LIMIT, SWIGLU_LIMIT)
        act = (u + 1.0) * (g * jax.nn.sigmoid(g * SWIGLU_ALPHA))
        return act @ w_dn[e] + b_dn[e]

    out = lax.map(expert_block, (buf_tok.reshape(nblk, MOE_BLOCK), blk_exp)).reshape(nblk * MOE_BLOCK, D_MODEL)
    return jnp.zeros((t, D_MODEL), h.dtype).at[tok_s].add(out[dest] * g_s[:, None].astype(h.dtype))


def _layer(x, pos, attend, chunked, g_mix, w_in, ln_g, ln_b, w_s, b_s, w_br_attn, w_br_sgu,
           w_out, g_ffn, w_router, b_router, w_gu, b_gu, w_dn, b_dn):
    n, l, _ = x.shape
    h = _rms(x, g_mix)
    q, k, v, u, vg, ga, gb = jnp.split(h @ w_in, PROJ_SPLITS, axis=-1)
    q = _rope(q.reshape(n, l, N_HEADS, HEAD_DIM), pos)
    k = _rope(k.reshape(n, l, N_HEADS, HEAD_DIM), pos)
    v = v.reshape(n, l, N_HEADS, HEAD_DIM)
    a = attend(q, k, v)
    vn = _layernorm(jax.nn.gelu(vg, approximate=False), ln_g, ln_b)
    if chunked:
        vr = vn.reshape(n, l // SGU_CHUNK, SGU_CHUNK, SGU_GROUPS, SGU_GROUP_DIM)
    else:
        vr = vn.reshape(n, l, SGU_GROUPS, SGU_GROUP_DIM)
    sg = jax.nn.gelu(u, approximate=False) * _spatial(vr, w_s, b_s).reshape(n, l, SGU_W)
    merged = jax.nn.sigmoid(ga) * (a @ w_br_attn) + jax.nn.sigmoid(gb) * (sg @ w_br_sgu)
    x = x + merged @ w_out
    ff = _moe(_rms(x, g_ffn).reshape(n * l, D_MODEL), w_router, b_router, w_gu, b_gu, w_dn, b_dn)
    x = x + ff.reshape(n, l, D_MODEL)
    return x, k, v, vn


def setup_inputs(seed: int = 0) -> dict:
    key = jax.random.key(seed)
    ks = jax.random.split(key, 24)
    f32 = jnp.float32
    n_pages = PAST_LEN // PAGE_SIZE
    n_used = DEC_BATCH * n_pages
    n_pool = n_used + n_used // 4
    nrm = lambda kk, shape, sc: jax.random.normal(kk, shape, f32) * sc
    page_table = jax.random.permutation(ks[0], n_pool)[:n_used].reshape(DEC_BATCH, n_pages).astype(jnp.int32)
    return {
        "x_prompt": nrm(ks[1], (BATCH, SEQ, D_MODEL), 1.0),
        "x_sample": nrm(ks[2], (DEC_BATCH, DEC_SEQ, D_MODEL), 1.0),
        "cache_k": nrm(ks[3], (DEPTH, n_pool, PAGE_SIZE, N_HEADS, HEAD_DIM), 1.0),
        "cache_v": nrm(ks[4], (DEPTH, n_pool, PAGE_SIZE, N_HEADS, HEAD_DIM), 1.0),
        "page_table": page_table,
        "g_mix": 1.0 + nrm(ks[5], (DEPTH, D_MODEL), 0.02),
        "w_in": nrm(ks[6], (DEPTH, D_MODEL, PROJ_W), D_MODEL ** -0.5),
        "ln_g": 1.0 + nrm(ks[7], (DEPTH, SGU_W), 0.02),
        "ln_b": nrm(ks[8], (DEPTH, SGU_W), 0.02),
        "w_s": nrm(ks[9], (DEPTH, SGU_GROUPS, SGU_CHUNK, SGU_CHUNK), SGU_CHUNK ** -0.5),
        "b_s": 1.0 + nrm(ks[10], (DEPTH, SGU_GROUPS, SGU_CHUNK), 0.02),
        "w_br_attn": nrm(ks[11], (DEPTH, ATTN_W, D_MODEL), ATTN_W ** -0.5),
        "w_br_sgu": nrm(ks[12], (DEPTH, SGU_W, D_MODEL), SGU_W ** -0.5),
        "w_out": nrm(ks[13], (DEPTH, D_MODEL, D_MODEL), D_MODEL ** -0.5),
        "g_ffn": 1.0 + nrm(ks[14], (DEPTH, D_MODEL), 0.02),
        "w_router": nrm(ks[15], (DEPTH, D_MODEL, N_EXPERTS), D_MODEL ** -0.5),
        "b_router": nrm(ks[16], (DEPTH, N_EXPERTS), 0.01),
        "w_gu": nrm(ks[17], (DEPTH, N_EXPERTS, D_MODEL, 2 * D_FF), D_MODEL ** -0.5),
        "b_gu": nrm(ks[18], (DEPTH, N_EXPERTS, 2 * D_FF), 0.01),
        "w_dn": nrm(ks[19], (DEPTH, N_EXPERTS, D_FF, D_MODEL), D_FF ** -0.5),
        "b_dn": nrm(ks[20], (DEPTH, N_EXPERTS, D_MODEL), 0.01),
        "g_final": 1.0 + nrm(ks[21], (D_MODEL,), 0.02),
    }


def reference(x_prompt, x_sample, cache_k, cache_v, page_table, g_mix, w_in, ln_g, ln_b, w_s, b_s,
              w_br_attn, w_br_sgu, w_out, g_ffn, w_router, b_router, w_gu, b_gu, w_dn, b_dn, g_final):
    past = page_table.shape[1] * PAGE_SIZE
    pos_p = jnp.arange(x_prompt.shape[1])
    pos_s = past + jnp.arange(x_sample.shape[1])
    page_ksum = jnp.sum(cache_k, axis=2, dtype=jnp.float32)
    xp, xs = x_prompt, x_sample
    kp_l, vp_l, ks_l, vs_l, sv_l = [], [], [], [], []
    for layer in range(DEPTH):
        lw = (g_mix[layer], w_in[layer], ln_g[layer], ln_b[layer], w_s[layer], b_s[layer],
              w_br_attn[layer], w_br_sgu[layer], w_out[layer], g_ffn[layer], w_router[layer],
              b_router[layer], w_gu[layer], b_gu[layer], w_dn[layer], b_dn[layer])
        xp, kp, vp, _ = _layer(xp, pos_p, _moba_prompt, True, *lw)
        attend_s = lambda q, k, v, layer=layer: _moba_sample(q, k, v, cache_k, cache_v,
                                                             page_ksum[layer], page_table, layer)
        xs, ks, vs, svn = _layer(xs, pos_s, attend_s, False, *lw)
        kp_l.append(kp); vp_l.append(vp); ks_l.append(ks); vs_l.append(vs); sv_l.append(svn)
    y_prompt = _rms(xp, g_final)
    y_sample = _rms(xs, g_final)
    new_k_prompt = jnp.stack(kp_l)
    new_v_prompt = jnp.stack(vp_l)
    new_k_sample = jnp.stack(ks_l)
    new_v_sample = jnp.stack(vs_l)
    new_sgu_v_sample = jnp.stack(sv_l)
    return (y_prompt, y_sample, new_k_prompt, new_v_prompt, new_k_sample, new_v_sample, new_sgu_v_sample)
```

```python
import functools

import jax
import jax.numpy as jnp
from jax import lax
from jax.experimental import pallas as pl
from jax.experimental.pallas import tpu as pltpu

N_HEADS = 8
HEAD_DIM = 128
MOBA_BLOCK = 256
MOBA_TOPK = 3
ROPE_THETA = 10000.0
PAGE_SIZE = 128
SGU_GROUPS = 8
SGU_GROUP_DIM = 128
SGU_CHUNK = 128
N_EXPERTS = 32
TOP_K = 4
MOE_BLOCK = 128
SWIGLU_LIMIT = 7.0
SWIGLU_ALPHA = 1.702
EPS = 1e-6
NEG = -1e30
N_PROJ = 7

F32 = jnp.float32
BF16 = jnp.bfloat16

VMEM_LIMIT_BYTES = 56 * 1024 * 1024
PROJ_ROWS = 256
POST_ROWS = 256
FINAL_ROWS = 256
PAGES_PER_STEP = 16


def _params(*sem):
    return pltpu.CompilerParams(dimension_semantics=sem, vmem_limit_bytes=VMEM_LIMIT_BYTES)


def _gelu(x):
    return 0.5 * x * (1.0 + lax.erf(x * 0.7071067811865476))


def _rms(x, g):
    return x * lax.rsqrt(jnp.mean(x * x, axis=-1, keepdims=True) + EPS) * g


def _proj_kernel(chunked, x_ref, cos_ref, sin_ref, gmix_ref, win_ref, lng_ref, lnb_ref,
                 wsp_ref, bsp_ref, wbs_ref, *out_refs):
    if chunked:
        q_ref, k_ref, v_ref, kb_ref, vb_ref, kmean_ref, sga_ref, msgu_ref = out_refs
    else:
        q_ref, k_ref, v_ref, vn_ref, sga_ref, msgu_ref = out_refs
    d = x_ref.shape[1]
    tm = x_ref.shape[0]
    hb = _rms(x_ref[...], gmix_ref[...]).astype(BF16)

    def proj(s):
        return jnp.dot(hb, win_ref[:, s * d:(s + 1) * d], preferred_element_type=F32)

    cos = cos_ref[...]
    sin = sin_ref[...]

    def rope(t, h):
        th = t[:, h * HEAD_DIM:(h + 1) * HEAD_DIM]
        return th * cos + pltpu.roll(th, HEAD_DIM // 2, axis=1) * sin

    q = proj(0)
    for h in range(N_HEADS):
        q_ref[:, h * HEAD_DIM:(h + 1) * HEAD_DIM] = rope(q, h)
    k = proj(1)
    for h in range(N_HEADS):
        kr = rope(k, h)
        cols = slice(h * HEAD_DIM, (h + 1) * HEAD_DIM)
        k_ref[:, cols] = kr
        if chunked:
            kb_ref[:, cols] = kr.astype(BF16)
            nb = tm // MOBA_BLOCK
            kmean_ref[:, 0, cols] = jnp.mean(kr.reshape(nb, MOBA_BLOCK, HEAD_DIM), axis=1)
    v = proj(2)
    v_ref[...] = v
    if chunked:
        vb_ref[...] = v.astype(BF16)

    gv = _gelu(proj(4))
    mu = jnp.mean(gv, axis=-1, keepdims=True)
    var = jnp.mean(jnp.square(gv - mu), axis=-1, keepdims=True)
    vn = (gv - mu) * lax.rsqrt(var + EPS) * lng_ref[...] + lnb_ref[...]
    gu = _gelu(proj(3))
    if chunked:
        vnb = vn.astype(BF16)
        for c in range(tm // SGU_CHUNK):
            rows = slice(c * SGU_CHUNK, (c + 1) * SGU_CHUNK)
            for g in range(SGU_GROUPS):
                cols = slice(g * SGU_GROUP_DIM, (g + 1) * SGU_GROUP_DIM)
                sp = jnp.dot(wsp_ref[g], vnb[rows, cols], preferred_element_type=F32) + bsp_ref[:, cols]
                msgu_ref[rows, cols] = gu[rows, cols] * sp
        sg = msgu_ref[...]
    else:
        vn_ref[...] = vn
        sg = gu * (wsp_ref[...] * vn + bsp_ref[...])
    sga_ref[...] = jax.nn.sigmoid(proj(5))
    sgb = jax.nn.sigmoid(proj(6))
    msgu_ref[...] = sgb * jnp.dot(sg.astype(BF16), wbs_ref[...], preferred_element_type=F32)


def _proj(x, cosf, sinf, g_mix, w_in_b, ln_g, ln_b, wsp, bsp, w_br_sgu_b, *, chunked, seq):
    t, d = x.shape
    tm = PROJ_ROWS if chunked else t
    assert t % tm == 0 and seq % tm == 0 and tm % MOBA_BLOCK == 0 or not chunked
    n_tab = seq // tm
    row = lambda i: (i, 0)
    full = lambda i: (0, 0)
    tab = lambda i: (i % n_tab, 0)
    in_specs = [
        pl.BlockSpec((tm, d), row),
        pl.BlockSpec((tm, HEAD_DIM), tab),
        pl.BlockSpec((tm, HEAD_DIM), tab),
        pl.BlockSpec((1, d), full),
        pl.BlockSpec((d, N_PROJ * d), full, pipeline_mode=pl.Buffered(1)),
        pl.BlockSpec((1, d), full),
        pl.BlockSpec((1, d), full),
        (pl.BlockSpec(wsp.shape, lambda i: (0, 0, 0)) if chunked else pl.BlockSpec((1, d), full)),
        pl.BlockSpec(bsp.shape, full),
        pl.BlockSpec((d, d), full, pipeline_mode=pl.Buffered(1)),
    ]
    f32_rows = jax.ShapeDtypeStruct((t, d), F32)
    bf_rows = jax.ShapeDtypeStruct((t, d), BF16)
    row_spec = pl.BlockSpec((tm, d), row)
    if chunked:
        nb = tm // MOBA_BLOCK
        out_shape = (f32_rows, f32_rows, f32_rows, bf_rows, bf_rows,
                     jax.ShapeDtypeStruct((t // MOBA_BLOCK, 1, d), F32), f32_rows, f32_rows)
        out_specs = (row_spec, row_spec, row_spec, row_spec, row_spec,
                     pl.BlockSpec((nb, 1, d), lambda i: (i, 0, 0)), row_spec, row_spec)
    else:
        out_shape = (f32_rows,) * 6
        out_specs = (row_spec,) * 6
    return pl.pallas_call(
        functools.partial(_proj_kernel, chunked),
        grid=(t // tm,),
        in_specs=in_specs,
        out_specs=out_specs,
        out_shape=out_shape,
        compiler_params=_params("parallel"),
        name="proj_prompt" if chunked else "proj_sample",
    )(x, cosf, sinf, g_mix, w_in_b, ln_g, ln_b, wsp, bsp, w_br_sgu_b)


def _select_bias(bsc, valid, n_sel):
    nb = bsc.shape[1]
    lane = lax.broadcasted_iota(jnp.int32, bsc.shape, 1)
    picked = jnp.zeros(bsc.shape, F32)
    for _ in range(n_sel):
        m = jnp.max(bsc, axis=1, keepdims=True)
        idx = jnp.min(jnp.where(bsc == m, lane, nb), axis=1, keepdims=True)
        hit = lane == idx
        picked = jnp.where(hit, 1.0, picked)
        bsc = jnp.where(hit, -jnp.inf, bsc)
    return jnp.where(valid, jnp.where(picked > 0.5, 0.0, NEG), NEG)


def _moba_kernel(n_sel, q_ref, kb_ref, vb_ref, kmean_ref, o_ref):
    c = pl.program_id(2)
    tq = q_ref.shape[1]
    nb = kmean_ref.shape[1]
    scale = HEAD_DIM ** -0.5
    q = q_ref[0]
    qb = q.astype(BF16)

    kmean = jnp.concatenate([kmean_ref[0], jnp.zeros((HEAD_DIM - nb, HEAD_DIM), F32)], axis=0)
    bsc = lax.dot_general(q, kmean, (((1,), (1,)), ((), ())),
                          precision=lax.Precision.HIGHEST, preferred_element_type=F32)
    lane = lax.broadcasted_iota(jnp.int32, (tq, HEAD_DIM), 1)
    valid = lane < c
    bias = _select_bias(jnp.where(valid, bsc, NEG), valid, n_sel)
    q_aug = jnp.concatenate([qb, bias.astype(BF16)], axis=1)
    blk_lane = lax.broadcasted_iota(jnp.int32, (MOBA_BLOCK, HEAD_DIM), 1)

    def flash(s, vblk, carry):
        m, l, acc = carry
        m_new = jnp.maximum(m, jnp.max(s, axis=1, keepdims=True))
        alpha = jnp.exp(m - m_new)
        p = jnp.exp(s - m_new)
        l = alpha * l + jnp.sum(p, axis=1, keepdims=True)
        acc = alpha * acc + jnp.dot(p.astype(BF16), vblk, preferred_element_type=F32)
        return m_new, l, acc

    def past(n, carry):
        rows = pl.ds(pl.multiple_of(n * MOBA_BLOCK, MOBA_BLOCK), MOBA_BLOCK)
        k_aug = jnp.concatenate([kb_ref[0, rows, :], jnp.where(blk_lane == n, 1.0, 0.0).astype(BF16)], axis=1)
        s = lax.dot_general(q_aug, k_aug, (((1,), (1,)), ((), ())), preferred_element_type=F32) * scale
        return flash(s, vb_ref[0, rows, :], carry)

    init = (jnp.full((tq, 1), -jnp.inf, F32), jnp.zeros((tq, 1), F32), jnp.zeros((tq, HEAD_DIM), F32))
    carry = lax.fori_loop(0, c, past, init)

    rows = pl.ds(pl.multiple_of(c * MOBA_BLOCK, MOBA_BLOCK), MOBA_BLOCK)
    s = lax.dot_general(qb, kb_ref[0, rows, :], (((1,), (1,)), ((), ())), preferred_element_type=F32) * scale
    qpos = lax.broadcasted_iota(jnp.int32, (tq, MOBA_BLOCK), 0)
    kpos = lax.broadcasted_iota(jnp.int32, (tq, MOBA_BLOCK), 1)
    s = jnp.where(kpos <= qpos, s, NEG)
    _, l, acc = flash(s, vb_ref[0, rows, :], carry)
    o_ref[0] = (acc / l).astype(o_ref.dtype)


def _moba_prompt(q, kb, vb, kmean):
    b, s, w = q.shape
    nb = kmean.shape[1]
    assert s % MOBA_BLOCK == 0 and nb == s // MOBA_BLOCK and nb <= HEAD_DIM
    tq = MOBA_BLOCK
    return pl.pallas_call(
        functools.partial(_moba_kernel, min(MOBA_TOPK, nb)),
        grid=(b, N_HEADS, s // tq),
        in_specs=[
            pl.BlockSpec((1, tq, HEAD_DIM), lambda bi, h, i: (bi, i, h)),
            pl.BlockSpec((1, s, HEAD_DIM), lambda bi, h, i: (bi, 0, h)),
            pl.BlockSpec((1, s, HEAD_DIM), lambda bi, h, i: (bi, 0, h)),
            pl.BlockSpec((1, nb, HEAD_DIM), lambda bi, h, i: (bi, 0, h)),
        ],
        out_specs=pl.BlockSpec((1, tq, HEAD_DIM), lambda bi, h, i: (bi, i, h)),
        out_shape=jax.ShapeDtypeStruct((b, s, w), BF16),
        compiler_params=_params("parallel", "parallel", "parallel"),
        name="moba_prompt",
    )(q, kb, vb, kmean)


def _page_dma(cache_ref, pt_ref, buf_ref, sem_ref, step, slot):
    n_pages = pt_ref.shape[1]
    flat = step * PAGES_PER_STEP
    return [
        pltpu.make_async_copy(
            cache_ref.at[pt_ref[(flat + j) // n_pages, (flat + j) % n_pages]],
            buf_ref.at[slot, j], sem_ref.at[slot])
        for j in range(PAGES_PER_STEP)
    ]


def _pagesum_kernel(pt_ref, cache_ref, o_ref, buf_ref, sem_ref):
    g = pl.program_id(0)
    slot = g % 2

    @pl.when(g == 0)
    def _():
        for cp in _page_dma(cache_ref, pt_ref, buf_ref, sem_ref, g, slot):
            cp.start()

    @pl.when(g + 1 < pl.num_programs(0))
    def _():
        for cp in _page_dma(cache_ref, pt_ref, buf_ref, sem_ref, g + 1, 1 - slot):
            cp.start()

    for cp in _page_dma(cache_ref, pt_ref, buf_ref, sem_ref, g, slot):
        cp.wait()
    for j in range(PAGES_PER_STEP):
        o_ref[j] = jnp.sum(buf_ref[slot, j], axis=0)


def _pagesum(cache_k_layer, page_table):
    db, n_pages = page_table.shape
    _, page, nh, hd = cache_k_layer.shape
    total = db * n_pages
    assert total % PAGES_PER_STEP == 0
    return pl.pallas_call(
        _pagesum_kernel,
        grid_spec=pltpu.PrefetchScalarGridSpec(
            num_scalar_prefetch=1,
            grid=(total // PAGES_PER_STEP,),
            in_specs=[pl.BlockSpec(memory_space=pl.ANY)],
            out_specs=pl.BlockSpec((PAGES_PER_STEP, nh, hd), lambda g, pt: (g, 0, 0)),
            scratch_shapes=[pltpu.VMEM((2, PAGES_PER_STEP, page, nh, hd), F32), pltpu.SemaphoreType.DMA((2,))],
        ),
        out_shape=jax.ShapeDtypeStruct((total, nh, hd), F32),
        compiler_params=_params("arbitrary"),
        name="pagesum",
    )(page_table, cache_k_layer)


def _blocksel_kernel(n_sel, ppb, q_ref, ps_ref, sel_ref):
    nblk = ps_ref.shape[1] // ppb
    ps = ps_ref[0]
    ps = ps.reshape(nblk, ppb, N_HEADS, HEAD_DIM)
    kmean = jnp.sum(ps, axis=1) / MOBA_BLOCK
    bsc = jnp.sum(kmean * q_ref[0][None], axis=-1)
    blk = lax.broadcasted_iota(jnp.int32, bsc.shape, 0)
    out = []
    for _ in range(n_sel):
        m = jnp.max(bsc, axis=0, keepdims=True)
        idx = jnp.min(jnp.where(bsc == m, blk, nblk), axis=0, keepdims=True)
        out.append(idx)
        bsc = jnp.where(blk == idx, -jnp.inf, bsc)
    sel_ref[0] = jnp.concatenate(out, axis=0)


def _blocksel(q_s, pagesums, n_sel, ppb):
    db, n_pages = pagesums.shape[:2]
    return pl.pallas_call(
        functools.partial(_blocksel_kernel, n_sel, ppb),
        grid=(db,),
        in_specs=[
            pl.BlockSpec((1, N_HEADS, HEAD_DIM), lambda n: (n, 0, 0)),
            pl.BlockSpec((1, n_pages, N_HEADS, HEAD_DIM), lambda n: (n, 0, 0, 0)),
        ],
        out_specs=pl.BlockSpec((1, n_sel, N_HEADS), lambda n: (n, 0, 0)),
        out_shape=jax.ShapeDtypeStruct((db, n_sel, N_HEADS), jnp.int32),
        compiler_params=_params("parallel"),
        name="blocksel",
    )(q_s, pagesums)


def _decode_dma(phys_ref, ck_ref, cv_ref, kbuf_ref, vbuf_ref, sem_ref, n, slot):
    n_pp = phys_ref.shape[2]
    copies = []
    for h in range(N_HEADS):
        for j in range(n_pp):
            page = phys_ref[n, h, j]
            rows = pl.ds(j * PAGE_SIZE, PAGE_SIZE)
            copies.append(pltpu.make_async_copy(ck_ref.at[page, :, h, :], kbuf_ref.at[slot, h, rows, :], sem_ref.at[0, slot]))
            copies.append(pltpu.make_async_copy(cv_ref.at[page, :, h, :], vbuf_ref.at[slot, h, rows, :], sem_ref.at[1, slot]))
    return copies


def _decode_kernel(phys_ref, q_ref, k_ref, v_ref, ck_ref, cv_ref, o_ref, kbuf_ref, vbuf_ref, sem_ref):
    n = pl.program_id(0)
    slot = n % 2
    scale = HEAD_DIM ** -0.5

    @pl.when(n == 0)
    def _():
        for cp in _decode_dma(phys_ref, ck_ref, cv_ref, kbuf_ref, vbuf_ref, sem_ref, n, slot):
            cp.start()

    @pl.when(n + 1 < pl.num_programs(0))
    def _():
        for cp in _decode_dma(phys_ref, ck_ref, cv_ref, kbuf_ref, vbuf_ref, sem_ref, n + 1, 1 - slot):
            cp.start()

    for cp in _decode_dma(phys_ref, ck_ref, cv_ref, kbuf_ref, vbuf_ref, sem_ref, n, slot):
        cp.wait()

    q = q_ref[0]
    s_new = jnp.sum(q * k_ref[0], axis=-1, keepdims=True) * scale
    for h in range(N_HEADS):
        qh = q[h:h + 1, :]
        s = jnp.sum(kbuf_ref[slot, h] * qh, axis=-1, keepdims=True) * scale
        sn = s_new[h:h + 1, :]
        m = jnp.maximum(jnp.max(s, axis=0, keepdims=True), sn)
        p = jnp.exp(s - m)
        pn = jnp.exp(sn - m)
        l = jnp.sum(p, axis=0, keepdims=True) + pn
        acc = jnp.sum(p * vbuf_ref[slot, h], axis=0, keepdims=True) + pn * v_ref[0, h:h + 1, :]
        o_ref[0, h:h + 1, :] = acc / l


def _decode(phys, q_s, k_s, v_s, cache_k_layer, cache_v_layer):
    db, _, n_pp = phys.shape
    nk = n_pp * PAGE_SIZE
    row = pl.BlockSpec((1, N_HEADS, HEAD_DIM), lambda n, ph: (n, 0, 0))
    return pl.pallas_call(
        _decode_kernel,
        grid_spec=pltpu.PrefetchScalarGridSpec(
            num_scalar_prefetch=1,
            grid=(db,),
            in_specs=[row, row, row, pl.BlockSpec(memory_space=pl.ANY), pl.BlockSpec(memory_space=pl.ANY)],
            out_specs=row,
            scratch_shapes=[
                pltpu.VMEM((2, N_HEADS, nk, HEAD_DIM), F32),
                pltpu.VMEM((2, N_HEADS, nk, HEAD_DIM), F32),
                pltpu.SemaphoreType.DMA((2, 2)),
            ],
        ),
        out_shape=jax.ShapeDtypeStruct((db, N_HEADS, HEAD_DIM), F32),
        compiler_params=_params("arbitrary"),
        name="decode_attn",
    )(phys, q_s, k_s, v_s, cache_k_layer, cache_v_layer)


def _post_kernel(a_ref, sga_ref, msgu_ref, x_ref, wba_ref, wo_ref, gffn_ref, wr_ref, br_ref,
                 x1_ref, h2_ref, idx_ref, gate_ref):
    merged = sga_ref[...] * jnp.dot(a_ref[...].astype(BF16), wba_ref[...], preferred_element_type=F32) + msgu_ref[...]
    x1 = x_ref[...] + jnp.dot(merged.astype(BF16), wo_ref[...], preferred_element_type=F32)
    x1_ref[...] = x1
    h2 = _rms(x1, gffn_ref[...])
    h2_ref[...] = h2
    logits = jnp.dot(h2, wr_ref[...], precision=lax.Precision.HIGHEST, preferred_element_type=F32) + br_ref[...]
    lane = lax.broadcasted_iota(jnp.int32, logits.shape, 1)
    vals, idxs = [], []
    for _ in range(TOP_K):
        m = jnp.max(logits, axis=1, keepdims=True)
        idx = jnp.min(jnp.where(logits == m, lane, N_EXPERTS), axis=1, keepdims=True)
        vals.append(m)
        idxs.append(idx)
        logits = jnp.where(lane == idx, -jnp.inf, logits)
    ex = [jnp.exp(v - vals[0]) for v in vals]
    den = ex[0] + ex[1] + ex[2] + ex[3]
    idx_ref[...] = jnp.concatenate(idxs, axis=1)
    gate_ref[...] = jnp.concatenate([e / den for e in ex], axis=1)


def _post(a, sga, msgu, x, w_br_attn_b, w_out_b, g_ffn, w_router, b_router):
    t, d = x.shape
    tm = min(POST_ROWS, t)
    assert t % tm == 0
    row = pl.BlockSpec((tm, d), lambda i: (i, 0))
    full = lambda i: (0, 0)
    small = pl.BlockSpec((tm, TOP_K), lambda i: (i, 0))
    return pl.pallas_call(
        _post_kernel,
        grid=(t // tm,),
        in_specs=[row, row, row, row,
                  pl.BlockSpec((d, d), full), pl.BlockSpec((d, d), full), pl.BlockSpec((1, d), full),
                  pl.BlockSpec((d, N_EXPERTS), full), pl.BlockSpec((1, N_EXPERTS), full)],
        out_specs=(row, row, small, small),
        out_shape=(jax.ShapeDtypeStruct((t, d), F32), jax.ShapeDtypeStruct((t, d), F32),
                   jax.ShapeDtypeStruct((t, TOP_K), jnp.int32), jax.ShapeDtypeStruct((t, TOP_K), F32)),
        compiler_params=_params("parallel"),
        name="post",
    )(a, sga, msgu, x, w_br_attn_b, w_out_b, g_ffn, w_router, b_router)


def _moe_rows(tok_ref, dst_ref, h_ref, o_ref, xbuf_ref, ybuf_ref, gsem_ref, ssem_ref, slot, gather):
    def body(r, _):
        if gather:
            pltpu.make_async_copy(h_ref.at[pl.ds(tok_ref[0, 0, r], 1), :], xbuf_ref.at[slot, pl.ds(r, 1), :],
                                  gsem_ref.at[slot]).start()
        else:
            pltpu.make_async_copy(ybuf_ref.at[slot, pl.ds(r, 1), :], o_ref.at[pl.ds(dst_ref[0, 0, r], 1), :],
                                  ssem_ref.at[slot]).start()
        return 0

    lax.fori_loop(0, MOE_BLOCK, body, 0, unroll=8)


def _moe_kernel(blk_exp_ref, tok_ref, tok_next_ref, dst_ref, h_ref, wgu_ref, bgu_ref, wdn_ref, bdn_ref,
                o_ref, xbuf_ref, ybuf_ref, gsem_ref, ssem_ref):
    del blk_exp_ref
    p = pl.program_id(0)
    nblk = pl.num_programs(0)
    slot = p % 2
    dff = wdn_ref.shape[1]
    rows = functools.partial(_moe_rows, h_ref=h_ref, o_ref=o_ref, xbuf_ref=xbuf_ref, ybuf_ref=ybuf_ref,
                             gsem_ref=gsem_ref, ssem_ref=ssem_ref)

    def wait_gather(s):
        pltpu.make_async_copy(h_ref.at[pl.ds(0, MOE_BLOCK), :], xbuf_ref.at[s], gsem_ref.at[s]).wait()

    def wait_scatter(s):
        pltpu.make_async_copy(ybuf_ref.at[s], o_ref.at[pl.ds(0, MOE_BLOCK), :], ssem_ref.at[s]).wait()

    @pl.when(p == 0)
    def _():
        rows(tok_ref, dst_ref, slot=slot, gather=True)

    @pl.when(p + 1 < nblk)
    def _():
        rows(tok_next_ref, dst_ref, slot=1 - slot, gather=True)

    wait_gather(slot)

    @pl.when(p >= 2)
    def _():
        wait_scatter(slot)

    xb = xbuf_ref[slot].astype(BF16)
    gu = jnp.dot(xb, wgu_ref[0], preferred_element_type=F32) + bgu_ref[0]
    g = jnp.minimum(gu[:, :dff], SWIGLU_LIMIT)
    u = jnp.clip(gu[:, dff:], -SWIGLU_LIMIT, SWIGLU_LIMIT)
    act = (u + 1.0) * (g * jax.nn.sigmoid(g * SWIGLU_ALPHA))
    ybuf_ref[slot] = jnp.dot(act.astype(BF16), wdn_ref[0], preferred_element_type=F32) + bdn_ref[0]
    rows(tok_ref, dst_ref, slot=slot, gather=False)

    @pl.when(p == nblk - 1)
    def _():
        wait_scatter(slot)

        @pl.when(nblk >= 2)
        def _():
            wait_scatter(1 - slot)


def _moe(h2, blk_exp, buf_tok, buf_dst, w_gu_b, b_gu, w_dn_b, b_dn, n_out_rows):
    t, d = h2.shape
    nblk = blk_exp.shape[0]
    dff2 = w_gu_b.shape[2]
    idx_spec = lambda f: pl.BlockSpec((1, 1, MOE_BLOCK), f, memory_space=pltpu.SMEM)
    return pl.pallas_call(
        _moe_kernel,
        grid_spec=pltpu.PrefetchScalarGridSpec(
            num_scalar_prefetch=1,
            grid=(nblk,),
            in_specs=[
                idx_spec(lambda p, be: (p, 0, 0)),
                idx_spec(lambda p, be: (jnp.minimum(p + 1, nblk - 1), 0, 0)),
                idx_spec(lambda p, be: (p, 0, 0)),
                pl.BlockSpec(memory_space=pl.ANY),
                pl.BlockSpec((1, d, dff2), lambda p, be: (be[p], 0, 0)),
                pl.BlockSpec((1, 1, dff2), lambda p, be: (be[p], 0, 0)),
                pl.BlockSpec((1, dff2 // 2, d), lambda p, be: (be[p], 0, 0)),
                pl.BlockSpec((1, 1, d), lambda p, be: (be[p], 0, 0)),
            ],
            out_specs=pl.BlockSpec(memory_space=pl.ANY),
            scratch_shapes=[
                pltpu.VMEM((2, MOE_BLOCK, d), F32),
                pltpu.VMEM((2, MOE_BLOCK, d), F32),
                pltpu.SemaphoreType.DMA((2,)),
                pltpu.SemaphoreType.DMA((2,)),
            ],
        ),
        out_shape=jax.ShapeDtypeStruct((n_out_rows, d), F32),
        compiler_params=_params("arbitrary"),
        name="moe",
    )(blk_exp, buf_tok, buf_tok, buf_dst, h2, w_gu_b, b_gu, w_dn_b, b_dn)


def _route(top_idx):
    t = top_idx.shape[0]
    m = t * TOP_K
    e_flat = top_idx.reshape(-1)
    order = jnp.argsort(e_flat)
    e_s = e_flat[order]
    counts = jnp.bincount(e_flat, length=N_EXPERTS)
    starts = jnp.cumsum(counts) - counts
    padded = (counts + MOE_BLOCK - 1) // MOE_BLOCK * MOE_BLOCK
    ends = jnp.cumsum(padded)
    pstarts = ends - padded
    dest = pstarts[e_s] + (jnp.arange(m) - starts[e_s])
    nblk = -(-(m + N_EXPERTS * (MOE_BLOCK - 1)) // MOE_BLOCK)
    n_buf = nblk * MOE_BLOCK
    buf_tok = jnp.zeros((n_buf,), jnp.int32).at[dest].set((order // TOP_K).astype(jnp.int32))
    real = jnp.zeros((n_buf,), jnp.int32).at[dest].set(1)
    buf_dst = jnp.zeros((n_buf,), jnp.int32).at[dest].set(order.astype(jnp.int32))
    spare = m + jnp.cumsum(1 - real) - 1
    buf_dst = jnp.where(real == 1, buf_dst, spare).astype(jnp.int32)
    blk_exp = jnp.clip(jnp.searchsorted(ends, jnp.arange(nblk) * MOE_BLOCK, side="right"), 0, N_EXPERTS - 1)
    shape = (nblk, 1, MOE_BLOCK)
    return blk_exp.astype(jnp.int32), buf_tok.reshape(shape), buf_dst.reshape(shape), n_buf


def _final_kernel(x1_ref, eo_ref, gate_ref, gfin_ref, y_ref):
    d = x1_ref.shape[1]
    gate = gate_ref[...]
    x2 = x1_ref[...]
    for j in range(TOP_K):
        x2 = x2 + gate[:, j:j + 1] * eo_ref[:, j * d:(j + 1) * d]
    y_ref[...] = _rms(x2, gfin_ref[...])


def _final(x1, eo4, gate, g_final, row0):
    t, d = x1.shape
    tm = min(FINAL_ROWS, t)
    assert t % tm == 0 and row0 % tm == 0
    off = row0 // tm
    return pl.pallas_call(
        _final_kernel,
        grid=(t // tm,),
        in_specs=[pl.BlockSpec((tm, d), lambda i: (i, 0)),
                  pl.BlockSpec((tm, TOP_K * d), lambda i: (i + off, 0)),
                  pl.BlockSpec((tm, TOP_K), lambda i: (i, 0)),
                  pl.BlockSpec((1, d), lambda i: (0, 0))],
        out_specs=pl.BlockSpec((tm, d), lambda i: (i, 0)),
        out_shape=jax.ShapeDtypeStruct((t, d), F32),
        compiler_params=_params("parallel"),
        name="final",
    )(x1, eo4, gate, g_final)


def _rope_tables(pos):
    inv = ROPE_THETA ** (-jnp.arange(0, HEAD_DIM, 2, dtype=F32) / HEAD_DIM)
    ang = pos.astype(F32)[:, None] * inv[None, :]
    cos, sin = jnp.cos(ang), jnp.sin(ang)
    return jnp.concatenate([cos, cos], axis=-1), jnp.concatenate([-sin, sin], axis=-1)


def kernel(x_prompt, x_sample, cache_k, cache_v, page_table, g_mix, w_in, ln_g, ln_b, w_s, b_s, w_br_attn, w_br_sgu, w_out, g_ffn, w_router, b_router, w_gu, b_gu, w_dn, b_dn, g_final):
    depth = w_in.shape[0]
    assert depth == 1, "the layer loop of this kernel is written for a single layer"
    layer = 0
    bsz, seq, d = x_prompt.shape
    db, dl, _ = x_sample.shape
    assert dl == 1
    n_pages = page_table.shape[1]
    past = n_pages * PAGE_SIZE
    ppb = MOBA_BLOCK // PAGE_SIZE
    assert past % MOBA_BLOCK == 0, "the sample group's own key block holds only the new row"
    nb_s = past // MOBA_BLOCK
    n_sel_s = min(MOBA_TOPK, nb_s)
    tp, ts = bsz * seq, db * dl

    row = lambda a: a[layer].reshape(1, -1)
    w_in_b = w_in[layer].astype(BF16)
    w_br_sgu_b = w_br_sgu[layer].astype(BF16)
    w_br_attn_b = w_br_attn[layer].astype(BF16)
    w_out_b = w_out[layer].astype(BF16)
    w_gu_b = w_gu[layer].astype(BF16)
    w_dn_b = w_dn[layer].astype(BF16)
    tril = jnp.tril(jnp.ones((SGU_CHUNK, SGU_CHUNK), F32))
    wsp_p = (w_s[layer] * tril).astype(BF16)
    bsp_p = jnp.repeat(b_s[layer].T, SGU_GROUP_DIM, axis=1)
    wsp_s = jnp.repeat(w_s[layer][:, 0, 0], SGU_GROUP_DIM).reshape(1, -1)
    bsp_s = jnp.repeat(b_s[layer][:, 0], SGU_GROUP_DIM).reshape(1, -1)

    cos_p, sin_p = _rope_tables(jnp.arange(seq))
    cos_s, sin_s = _rope_tables(jnp.repeat(past + jnp.arange(dl), db))

    proj = functools.partial(_proj, g_mix=row(g_mix), w_in_b=w_in_b, ln_g=row(ln_g), ln_b=row(ln_b),
                             w_br_sgu_b=w_br_sgu_b)
    q_p, k_p, v_p, kb_p, vb_p, kmean_p, sga_p, msgu_p = proj(
        x_prompt.reshape(tp, d), cos_p, sin_p, wsp=wsp_p, bsp=bsp_p, chunked=True, seq=seq)
    q_s, k_s, v_s, vn_s, sga_s, msgu_s = proj(
        x_sample.reshape(ts, d), cos_s, sin_s, wsp=wsp_s, bsp=bsp_s, chunked=False, seq=ts)

    a_p = _moba_prompt(q_p.reshape(bsz, seq, d), kb_p.reshape(bsz, seq, d), vb_p.reshape(bsz, seq, d),
                       kmean_p.reshape(bsz, seq // MOBA_BLOCK, d))

    heads = lambda a: a.reshape(db, N_HEADS, HEAD_DIM)
    pagesums = _pagesum(cache_k[layer], page_table).reshape(db, n_pages, N_HEADS, HEAD_DIM)
    sel = _blocksel(heads(q_s), pagesums, n_sel_s, ppb)
    logical = sel.transpose(0, 2, 1)[..., None] * ppb + jnp.arange(ppb)
    phys = jnp.take_along_axis(page_table[:, None, :], logical.reshape(db, N_HEADS, -1), axis=2)
    a_s = _decode(phys.astype(jnp.int32), heads(q_s), heads(k_s), heads(v_s), cache_k[layer], cache_v[layer])

    post = functools.partial(_post, w_br_attn_b=w_br_attn_b, w_out_b=w_out_b, g_ffn=row(g_ffn),
                             w_router=w_router[layer], b_router=row(b_router))
    x1_p, h2_p, idx_p, gate_p = post(a_p.reshape(tp, d), sga_p, msgu_p, x_prompt.reshape(tp, d))
    x1_s, h2_s, idx_s, gate_s = post(a_s.reshape(ts, d), sga_s, msgu_s, x_sample.reshape(ts, d))

    h2 = jnp.concatenate([h2_p, h2_s], axis=0)
    blk_exp, buf_tok, buf_dst, n_buf = _route(jnp.concatenate([idx_p, idx_s], axis=0))
    eo = _moe(h2, blk_exp, buf_tok, buf_dst, w_gu_b, b_gu[layer][:, None, :], w_dn_b, b_dn[layer][:, None, :], n_buf)
    eo4 = eo.reshape(n_buf // TOP_K, TOP_K * d)

    y_p = _final(x1_p, eo4, gate_p, g_final.reshape(1, -1), 0)
    y_s = _final(x1_s, eo4, gate_s, g_final.reshape(1, -1), tp)

    return (y_p.reshape(bsz, seq, d), y_s.reshape(db, dl, d),
            k_p.reshape(depth, bsz, seq, N_HEADS, HEAD_DIM), v_p.reshape(depth, bsz, seq, N_HEADS, HEAD_DIM),
            k_s.reshape(depth, db, dl, N_HEADS, HEAD_DIM), v_s.reshape(depth, db, dl, N_HEADS, HEAD_DIM),
            vn_s.reshape(depth, db, dl, d))
```

```python
import functools

import jax
import jax.numpy as jnp
from jax import lax
from jax.experimental import pallas as pl
from jax.experimental.pallas import tpu as pltpu

N_HEADS = 8
HEAD_DIM = 128
MOBA_BLOCK = 256
MOBA_TOPK = 3
ROPE_THETA = 10000.0
PAGE_SIZE = 128
SGU_GROUPS = 8
SGU_GROUP_DIM = 128
SGU_CHUNK = 128
N_EXPERTS = 32
TOP_K = 4
MOE_BLOCK = 128
SWIGLU_LIMIT = 7.0
SWIGLU_ALPHA = 1.702
EPS = 1e-6
NEG = -1e30
N_PROJ = 7

F32 = jnp.float32
BF16 = jnp.bfloat16

VMEM_LIMIT_BYTES = 56 * 1024 * 1024
PROJ_ROWS = 256
POST_ROWS = 256
FINAL_ROWS = 256
MAX_PAGES_PER_STEP = 16
MOBA_GROUP = 4
ROW_UNROLL = 8


def _params(*sem):
    return pltpu.CompilerParams(dimension_semantics=sem, vmem_limit_bytes=VMEM_LIMIT_BYTES)


def _gelu(x):
    return 0.5 * x * (1.0 + lax.erf(x * 0.7071067811865476))


def _rms(x, g):
    return x * lax.rsqrt(jnp.mean(x * x, axis=-1, keepdims=True) + EPS) * g


def _proj_kernel(chunked, x_ref, cos_ref, sin_ref, gmix_ref, win_ref, lng_ref, lnb_ref,
                 wsp_ref, bsp_ref, wbs_ref, *out_refs):
    if chunked:
        q_ref, k_ref, v_ref, kb_ref, vb_ref, kmean_ref, sga_ref, msgu_ref = out_refs
    else:
        q_ref, k_ref, v_ref, vn_ref, sga_ref, msgu_ref = out_refs
    d = x_ref.shape[1]
    tm = x_ref.shape[0]
    hb = _rms(x_ref[...], gmix_ref[...]).astype(BF16)

    def proj(s):
        return jnp.dot(hb, win_ref[:, s * d:(s + 1) * d], preferred_element_type=F32)

    cos = cos_ref[...]
    sin = sin_ref[...]

    def rope(t, h):
        th = t[:, h * HEAD_DIM:(h + 1) * HEAD_DIM]
        return th * cos + pltpu.roll(th, HEAD_DIM // 2, axis=1) * sin

    q = proj(0)
    for h in range(N_HEADS):
        q_ref[:, h * HEAD_DIM:(h + 1) * HEAD_DIM] = rope(q, h)
    k = proj(1)
    for h in range(N_HEADS):
        kr = rope(k, h)
        cols = slice(h * HEAD_DIM, (h + 1) * HEAD_DIM)
        k_ref[:, cols] = kr
        if chunked:
            kb_ref[:, cols] = kr.astype(BF16)
            nb = tm // MOBA_BLOCK
            kmean_ref[:, 0, cols] = jnp.mean(kr.reshape(nb, MOBA_BLOCK, HEAD_DIM), axis=1)
    v = proj(2)
    v_ref[...] = v
    if chunked:
        vb_ref[...] = v.astype(BF16)

    gv = _gelu(proj(4))
    mu = jnp.mean(gv, axis=-1, keepdims=True)
    var = jnp.mean(jnp.square(gv - mu), axis=-1, keepdims=True)
    vn = (gv - mu) * lax.rsqrt(var + EPS) * lng_ref[...] + lnb_ref[...]
    gu = _gelu(proj(3))
    if chunked:
        vnb = vn.astype(BF16)
        for c in range(tm // SGU_CHUNK):
            rows = slice(c * SGU_CHUNK, (c + 1) * SGU_CHUNK)
            for g in range(SGU_GROUPS):
                cols = slice(g * SGU_GROUP_DIM, (g + 1) * SGU_GROUP_DIM)
                sp = jnp.dot(wsp_ref[g], vnb[rows, cols], preferred_element_type=F32) + bsp_ref[:, cols]
                msgu_ref[rows, cols] = gu[rows, cols] * sp
        sg = msgu_ref[...]
    else:
        vn_ref[...] = vn
        sg = gu * (wsp_ref[...] * vn + bsp_ref[...])
    sga_ref[...] = jax.nn.sigmoid(proj(5))
    sgb = jax.nn.sigmoid(proj(6))
    msgu_ref[...] = sgb * jnp.dot(sg.astype(BF16), wbs_ref[...], preferred_element_type=F32)


def _proj(x, cosf, sinf, g_mix, w_in_b, ln_g, ln_b, wsp, bsp, w_br_sgu_b, *, chunked, seq):
    t, d = x.shape
    tm = PROJ_ROWS if chunked else t
    assert t % tm == 0 and seq % tm == 0 and tm % MOBA_BLOCK == 0 or not chunked
    n_tab = seq // tm
    row = lambda i: (i, 0)
    full = lambda i: (0, 0)
    tab = lambda i: (i % n_tab, 0)
    in_specs = [
        pl.BlockSpec((tm, d), row),
        pl.BlockSpec((tm, HEAD_DIM), tab),
        pl.BlockSpec((tm, HEAD_DIM), tab),
        pl.BlockSpec((1, d), full),
        pl.BlockSpec((d, N_PROJ * d), full, pipeline_mode=pl.Buffered(1)),
        pl.BlockSpec((1, d), full),
        pl.BlockSpec((1, d), full),
        (pl.BlockSpec(wsp.shape, lambda i: (0, 0, 0)) if chunked else pl.BlockSpec((1, d), full)),
        pl.BlockSpec(bsp.shape, full),
        pl.BlockSpec((d, d), full, pipeline_mode=pl.Buffered(1)),
    ]
    f32_rows = jax.ShapeDtypeStruct((t, d), F32)
    bf_rows = jax.ShapeDtypeStruct((t, d), BF16)
    row_spec = pl.BlockSpec((tm, d), row)
    if chunked:
        nb = tm // MOBA_BLOCK
        out_shape = (f32_rows, f32_rows, f32_rows, bf_rows, bf_rows,
                     jax.ShapeDtypeStruct((t // MOBA_BLOCK, 1, d), F32), f32_rows, f32_rows)
        out_specs = (row_spec, row_spec, row_spec, row_spec, row_spec,
                     pl.BlockSpec((nb, 1, d), lambda i: (i, 0, 0)), row_spec, row_spec)
    else:
        out_shape = (f32_rows,) * 6
        out_specs = (row_spec,) * 6
    return pl.pallas_call(
        functools.partial(_proj_kernel, chunked),
        grid=(t // tm,),
        in_specs=in_specs,
        out_specs=out_specs,
        out_shape=out_shape,
        compiler_params=_params("parallel"),
        name="proj_prompt" if chunked else "proj_sample",
    )(x, cosf, sinf, g_mix, w_in_b, ln_g, ln_b, wsp, bsp, w_br_sgu_b)


def _topk_rows(score, valid, n_sel):
    nb = score.shape[0]
    blk = lax.broadcasted_iota(jnp.int32, score.shape, 0).astype(F32)
    work = jnp.where(valid, score, NEG)
    picked = jnp.zeros(score.shape, F32)
    for _ in range(n_sel):
        m = jnp.max(work, axis=0, keepdims=True)
        idx = jnp.min(jnp.where(work == m, blk, float(nb)), axis=0, keepdims=True)
        hit = blk == idx
        picked = jnp.where(hit, 1.0, picked)
        work = jnp.where(hit, -jnp.inf, work)
    return jnp.where(valid, picked, 0.0)


def _sum_rows(x):
    while x.shape[0] > 1:
        half = x.shape[0] // 2
        assert 2 * half == x.shape[0]
        x = x[:half] + x[half:]
    return x[0]


def _page_copies(cache_ref, pt_ref, buf_ref, sem_ref, chunk, slot):
    pps = buf_ref.shape[1]
    n_pages = pt_ref.shape[1]
    last = pt_ref.shape[0] * n_pages - 1
    copies = []
    for j in range(pps):
        page = jnp.minimum(chunk * pps + j, last)
        copies.append(pltpu.make_async_copy(cache_ref.at[pt_ref[page // n_pages, page % n_pages]],
                                            buf_ref.at[slot, j], sem_ref.at[slot]))
    return copies


def _tile_of_step(i, n):
    return jnp.where(i % 2 == 0, i // 2, n - 1 - i // 2)


def _moba_kernel(n_sel, group, pt_ref, q_ref, kb_ref, vb_ref, kmean_ref, cache_ref, o_ref, ps_ref, buf_ref, sem_ref):
    n_q = pl.num_programs(2)
    step = (pl.program_id(0) * pl.num_programs(1) + pl.program_id(1)) * n_q + pl.program_id(2)
    n_steps = pl.num_programs(0) * pl.num_programs(1) * n_q
    slot = step % 2

    @pl.when(step == 0)
    def _():
        for cp in _page_copies(cache_ref, pt_ref, buf_ref, sem_ref, step, slot):
            cp.start()

    @pl.when(step + 1 < n_steps)
    def _():
        for cp in _page_copies(cache_ref, pt_ref, buf_ref, sem_ref, step + 1, 1 - slot):
            cp.start()

    c = _tile_of_step(pl.program_id(2), n_q)
    tq = q_ref.shape[1]
    nb = kmean_ref.shape[1]
    q = q_ref[0]
    qs = (q * (HEAD_DIM ** -0.5 * 1.4426950408889634)).astype(BF16)

    bsc = lax.dot_general(kmean_ref[0], q, (((1,), (1,)), ((), ())),
                          precision=lax.Precision.HIGHEST, preferred_element_type=F32)
    valid = lax.broadcasted_iota(jnp.int32, (nb, tq), 0) < c
    bias_t = jnp.where(_topk_rows(bsc, valid, n_sel) > 0.5, 0.0, NEG)
    bias = jnp.concatenate([bias_t, jnp.zeros((HEAD_DIM - nb, tq), F32)], axis=0).T
    q_aug = jnp.concatenate([qs, bias.astype(BF16)], axis=1)

    def flash(s, vblk, carry):
        m, l, acc = carry
        m_new = jnp.maximum(m, jnp.max(s, axis=1, keepdims=True))
        alpha = jnp.exp2(m - m_new)
        p = jnp.exp2(s - m_new)
        l = alpha * l + jnp.sum(p, axis=1, keepdims=True)
        acc = alpha * acc + jnp.dot(p.astype(BF16), vblk, preferred_element_type=F32)
        return m_new, l, acc

    gk = group * MOBA_BLOCK
    key_blk = lax.broadcasted_iota(jnp.int32, (gk, HEAD_DIM), 0) // MOBA_BLOCK
    lane = lax.broadcasted_iota(jnp.int32, (gk, HEAD_DIM), 1)

    def past(gi, carry):
        rows = pl.ds(pl.multiple_of(gi * gk, gk), gk)
        onehot = jnp.where(lane == key_blk + gi * group, 1.0, 0.0).astype(BF16)
        k_aug = jnp.concatenate([kb_ref[0, rows, :], onehot], axis=1)
        s = lax.dot_general(q_aug, k_aug, (((1,), (1,)), ((), ())), preferred_element_type=F32)
        return flash(s, vb_ref[0, rows, :], carry)

    init = (jnp.full((tq, 1), -jnp.inf, F32), jnp.zeros((tq, 1), F32), jnp.zeros((tq, HEAD_DIM), F32))
    carry = lax.fori_loop(0, (c + group - 1) // group, past, init)

    rows = pl.ds(pl.multiple_of(c * MOBA_BLOCK, MOBA_BLOCK), MOBA_BLOCK)
    s = lax.dot_general(qs, kb_ref[0, rows, :], (((1,), (1,)), ((), ())), preferred_element_type=F32)
    qpos = lax.broadcasted_iota(jnp.int32, (tq, MOBA_BLOCK), 0)
    kpos = lax.broadcasted_iota(jnp.int32, (tq, MOBA_BLOCK), 1)
    s = jnp.where(kpos <= qpos, s, NEG)
    _, l, acc = flash(s, vb_ref[0, rows, :], carry)
    o_ref[0] = (acc / l).astype(o_ref.dtype)

    for cp in _page_copies(cache_ref, pt_ref, buf_ref, sem_ref, step, slot):
        cp.wait()
    for j in range(buf_ref.shape[1]):
        ps_ref[j] = _sum_rows(buf_ref[slot, j])


def _moba_prompt(q, kb, vb, kmean, cache_k_layer, page_table):
    b, s, w = q.shape
    nb = kmean.shape[1]
    assert s % MOBA_BLOCK == 0 and nb == s // MOBA_BLOCK and nb <= HEAD_DIM
    tq = MOBA_BLOCK
    n_q = s // tq
    group = max(g for g in (1, 2, 4) if g <= MOBA_GROUP and nb % g == 0)
    _, page, nh, hd = cache_k_layer.shape
    total = page_table.shape[0] * page_table.shape[1]
    n_steps = b * N_HEADS * n_q
    pps = -(-total // n_steps)
    assert pps <= MAX_PAGES_PER_STEP, "page buffers for this many pages per step do not fit VMEM"
    tile = lambda bi, h, i, pt: (bi, _tile_of_step(i, n_q), h)
    whole = lambda bi, h, i, pt: (bi, 0, h)
    att, sums = pl.pallas_call(
        functools.partial(_moba_kernel, min(MOBA_TOPK, nb), group),
        grid_spec=pltpu.PrefetchScalarGridSpec(
            num_scalar_prefetch=1,
            grid=(b, N_HEADS, n_q),
            in_specs=[
                pl.BlockSpec((1, tq, HEAD_DIM), tile),
                pl.BlockSpec((1, s, HEAD_DIM), whole),
                pl.BlockSpec((1, s, HEAD_DIM), whole),
                pl.BlockSpec((1, nb, HEAD_DIM), whole),
                pl.BlockSpec(memory_space=pl.ANY),
            ],
            out_specs=(pl.BlockSpec((1, tq, HEAD_DIM), tile),
                       pl.BlockSpec((pps, nh, hd), lambda bi, h, i, pt: ((bi * N_HEADS + h) * n_q + i, 0, 0))),
            scratch_shapes=[pltpu.VMEM((2, pps, page, nh, hd), F32), pltpu.SemaphoreType.DMA((2,))],
        ),
        out_shape=(jax.ShapeDtypeStruct((b, s, w), BF16), jax.ShapeDtypeStruct((n_steps * pps, nh, hd), F32)),
        compiler_params=_params("arbitrary", "arbitrary", "arbitrary"),
        name="moba_prompt",
    )(page_table, q, kb, vb, kmean, cache_k_layer)
    return att, sums[:total]


def _blocksel_kernel(n_sel, ppb, q_ref, ps_ref, sel_ref):
    nblk = ps_ref.shape[1] // ppb
    ps = ps_ref[0]
    ps = ps.reshape(nblk, ppb, N_HEADS, HEAD_DIM)
    kmean = jnp.sum(ps, axis=1) / MOBA_BLOCK
    bsc = jnp.sum(kmean * q_ref[0][None], axis=-1)
    blk = lax.broadcasted_iota(jnp.int32, bsc.shape, 0)
    out = []
    for _ in range(n_sel):
        m = jnp.max(bsc, axis=0, keepdims=True)
        idx = jnp.min(jnp.where(bsc == m, blk, nblk), axis=0, keepdims=True)
        out.append(idx)
        bsc = jnp.where(blk == idx, -jnp.inf, bsc)
    sel_ref[0] = jnp.concatenate(out, axis=0)


def _blocksel(q_s, pagesums, n_sel, ppb):
    db, n_pages = pagesums.shape[:2]
    return pl.pallas_call(
        functools.partial(_blocksel_kernel, n_sel, ppb),
        grid=(db,),
        in_specs=[
            pl.BlockSpec((1, N_HEADS, HEAD_DIM), lambda n: (n, 0, 0)),
            pl.BlockSpec((1, n_pages, N_HEADS, HEAD_DIM), lambda n: (n, 0, 0, 0)),
        ],
        out_specs=pl.BlockSpec((1, n_sel, N_HEADS), lambda n: (n, 0, 0)),
        out_shape=jax.ShapeDtypeStruct((db, n_sel, N_HEADS), jnp.int32),
        compiler_params=_params("parallel"),
        name="blocksel",
    )(q_s, pagesums)


def _decode_dma(phys_ref, ck_ref, cv_ref, kbuf_ref, vbuf_ref, sem_ref, n, slot):
    n_pp = phys_ref.shape[2]
    copies = []
    for h in range(N_HEADS):
        for j in range(n_pp):
            page = phys_ref[n, h, j]
            rows = pl.ds(j * PAGE_SIZE, PAGE_SIZE)
            copies.append(pltpu.make_async_copy(ck_ref.at[page, :, h, :], kbuf_ref.at[slot, h, rows, :], sem_ref.at[0, slot]))
            copies.append(pltpu.make_async_copy(cv_ref.at[page, :, h, :], vbuf_ref.at[slot, h, rows, :], sem_ref.at[1, slot]))
    return copies


def _decode_kernel(phys_ref, q_ref, k_ref, v_ref, ck_ref, cv_ref, o_ref, kbuf_ref, vbuf_ref, sem_ref):
    n = pl.program_id(0)
    slot = n % 2
    scale = HEAD_DIM ** -0.5

    @pl.when(n == 0)
    def _():
        for cp in _decode_dma(phys_ref, ck_ref, cv_ref, kbuf_ref, vbuf_ref, sem_ref, n, slot):
            cp.start()

    @pl.when(n + 1 < pl.num_programs(0))
    def _():
        for cp in _decode_dma(phys_ref, ck_ref, cv_ref, kbuf_ref, vbuf_ref, sem_ref, n + 1, 1 - slot):
            cp.start()

    for cp in _decode_dma(phys_ref, ck_ref, cv_ref, kbuf_ref, vbuf_ref, sem_ref, n, slot):
        cp.wait()

    q = q_ref[0]
    s_new = jnp.sum(q * k_ref[0], axis=-1, keepdims=True) * scale
    for h in range(N_HEADS):
        qh = q[h:h + 1, :]
        s = jnp.sum(kbuf_ref[slot, h] * qh, axis=-1, keepdims=True) * scale
        sn = s_new[h:h + 1, :]
        m = jnp.maximum(jnp.max(s, axis=0, keepdims=True), sn)
        p = jnp.exp(s - m)
        pn = jnp.exp(sn - m)
        l = jnp.sum(p, axis=0, keepdims=True) + pn
        acc = jnp.sum(p * vbuf_ref[slot, h], axis=0, keepdims=True) + pn * v_ref[0, h:h + 1, :]
        o_ref[0, h:h + 1, :] = acc / l


def _decode(phys, q_s, k_s, v_s, cache_k_layer, cache_v_layer):
    db, _, n_pp = phys.shape
    nk = n_pp * PAGE_SIZE
    row = pl.BlockSpec((1, N_HEADS, HEAD_DIM), lambda n, ph: (n, 0, 0))
    return pl.pallas_call(
        _decode_kernel,
        grid_spec=pltpu.PrefetchScalarGridSpec(
            num_scalar_prefetch=1,
            grid=(db,),
            in_specs=[row, row, row, pl.BlockSpec(memory_space=pl.ANY), pl.BlockSpec(memory_space=pl.ANY)],
            out_specs=row,
            scratch_shapes=[
                pltpu.VMEM((2, N_HEADS, nk, HEAD_DIM), F32),
                pltpu.VMEM((2, N_HEADS, nk, HEAD_DIM), F32),
                pltpu.SemaphoreType.DMA((2, 2)),
            ],
        ),
        out_shape=jax.ShapeDtypeStruct((db, N_HEADS, HEAD_DIM), F32),
        compiler_params=_params("arbitrary"),
        name="decode_attn",
    )(phys, q_s, k_s, v_s, cache_k_layer, cache_v_layer)


def _post_kernel(a_ref, sga_ref, msgu_ref, x_ref, wba_ref, wo_ref, gffn_ref, wrt_ref, br_ref, cnt_in_ref,
                 x1_ref, h2_ref, idx_ref, gate_ref, rank_ref, cnt_ref):
    tm = x_ref.shape[0]

    @pl.when(pl.program_id(0) == 0)
    def _():
        cnt_ref[...] = cnt_in_ref[...]

    merged = sga_ref[...] * jnp.dot(a_ref[...].astype(BF16), wba_ref[...], preferred_element_type=F32) + msgu_ref[...]
    x1 = x_ref[...] + jnp.dot(merged.astype(BF16), wo_ref[...], preferred_element_type=F32)
    x1_ref[...] = x1
    h2 = _rms(x1, gffn_ref[...])
    h2_ref[...] = h2
    logits = lax.dot_general(wrt_ref[...], h2, (((1,), (1,)), ((), ())),
                             precision=lax.Precision.HIGHEST, preferred_element_type=F32) + br_ref[...]
    exp_id = lax.broadcasted_iota(jnp.int32, logits.shape, 0).astype(F32)
    vals, hits = [], []
    for _ in range(TOP_K):
        m = jnp.max(logits, axis=0, keepdims=True)
        idx = jnp.min(jnp.where(logits == m, exp_id, float(N_EXPERTS)), axis=0, keepdims=True)
        hit = exp_id == idx
        vals.append(m)
        hits.append((idx, hit))
        logits = jnp.where(hit, -jnp.inf, logits)
    ex = [jnp.exp(v - vals[0]) for v in vals]
    den = ex[0] + ex[1] + ex[2] + ex[3]
    gate_ref[...] = jnp.concatenate([e / den for e in ex], axis=0)
    idx_ref[...] = jnp.concatenate([i for i, _ in hits], axis=0).astype(jnp.int32)

    sel = jnp.zeros(logits.shape, F32)
    for _, hit in hits:
        sel = jnp.where(hit, 1.0, sel)
    earlier = (lax.broadcasted_iota(jnp.int32, (tm, tm), 0) < lax.broadcasted_iota(jnp.int32, (tm, tm), 1))
    before = jnp.dot(sel.astype(BF16), jnp.where(earlier, 1.0, 0.0).astype(BF16), preferred_element_type=F32)
    before = before + cnt_ref[:, 0:1]
    rank_ref[...] = jnp.concatenate(
        [jnp.sum(jnp.where(hit, before, 0.0), axis=0, keepdims=True) for _, hit in hits], axis=0).astype(jnp.int32)
    cnt_ref[...] = cnt_ref[...] + jnp.sum(sel, axis=1, keepdims=True)


def _post(a, sga, msgu, x, counts_in, w_br_attn_b, w_out_b, g_ffn, w_router_t, b_router_col):
    t, d = x.shape
    tm = min(POST_ROWS, t)
    assert t % tm == 0
    row = pl.BlockSpec((tm, d), lambda i: (i, 0))
    full = lambda i: (0, 0)
    tok = pl.BlockSpec((TOP_K, tm), lambda i: (0, i))
    cnt = pl.BlockSpec((N_EXPERTS, HEAD_DIM), full)
    return pl.pallas_call(
        _post_kernel,
        grid=(t // tm,),
        in_specs=[row, row, row, row,
                  pl.BlockSpec((d, d), full), pl.BlockSpec((d, d), full), pl.BlockSpec((1, d), full),
                  pl.BlockSpec((N_EXPERTS, d), full), pl.BlockSpec((N_EXPERTS, 1), full), cnt],
        out_specs=(row, row, tok, tok, tok, cnt),
        out_shape=(jax.ShapeDtypeStruct((t, d), F32), jax.ShapeDtypeStruct((t, d), F32),
                   jax.ShapeDtypeStruct((TOP_K, t), jnp.int32), jax.ShapeDtypeStruct((TOP_K, t), F32),
                   jax.ShapeDtypeStruct((TOP_K, t), jnp.int32), jax.ShapeDtypeStruct((N_EXPERTS, HEAD_DIM), F32)),
        compiler_params=_params("arbitrary"),
        name="post",
    )(a, sga, msgu, x, w_br_attn_b, w_out_b, g_ffn, w_router_t, b_router_col, counts_in)


def _dest_tiles(dest, tm):
    t = dest.shape[1]
    return dest.reshape(TOP_K, t // tm, tm).transpose(1, 0, 2).reshape(t // tm, 1, TOP_K * tm)


def _dispatch_kernel(last_blk_ref, tail_ref, dest_ref, hp_ref, hs_ref, xs_ref, zeros_ref, sem_ref):
    i = pl.program_id(0)
    n_prompt = pl.num_programs(0) - 1

    @pl.when(i == 0)
    def _():
        zeros_ref[...] = jnp.zeros(zeros_ref.shape, zeros_ref.dtype)

        def fill(blk):
            rows = pl.ds(pl.multiple_of(blk * MOE_BLOCK, MOE_BLOCK), MOE_BLOCK)
            return pltpu.make_async_copy(zeros_ref, xs_ref.at[rows, :], sem_ref.at[1])

        for start in (True, False):
            act = (lambda cp: cp.start()) if start else (lambda cp: cp.wait())
            for e in range(N_EXPERTS):
                @pl.when(last_blk_ref[e] >= 0)
                def _():
                    act(fill(last_blk_ref[e]))

            def tail(blk, _):
                act(fill(blk))
                return 0

            lax.fori_loop(tail_ref[0], xs_ref.shape[0] // MOE_BLOCK, tail, 0)

    def copy_rows(h_ref):
        tm = h_ref.shape[0]

        def body(r8, _):
            r0 = pl.multiple_of(r8 * ROW_UNROLL, ROW_UNROLL)
            for k in range(ROW_UNROLL):
                for j in range(TOP_K):
                    pltpu.make_async_copy(h_ref.at[pl.ds(r0 + k, 1), :],
                                          xs_ref.at[pl.ds(dest_ref[0, 0, j * tm + r0 + k], 1), :], sem_ref.at[0]).start()
            return 0

        lax.fori_loop(0, tm // ROW_UNROLL, body, 0)
        for _ in range(TOP_K):
            pltpu.make_async_copy(h_ref, xs_ref.at[pl.ds(0, tm), :], sem_ref.at[0]).wait()

    @pl.when(i < n_prompt)
    def _():
        copy_rows(hp_ref)

    @pl.when(i == n_prompt)
    def _():
        copy_rows(hs_ref)


def _dispatch(h2_p, h2_s, dest_p, dest_s, last_blk, tail_blk, n_buf):
    tp, d = h2_p.shape
    ts = h2_s.shape[0]
    tm = POST_ROWS
    assert tp % tm == 0 and ts <= tm and ts % ROW_UNROLL == 0
    n_prompt = tp // tm
    tile_s = jnp.pad(_dest_tiles(dest_s, ts), ((0, 0), (0, 0), (0, TOP_K * (tm - ts))))
    tiles = jnp.concatenate([_dest_tiles(dest_p, tm), tile_s], axis=0)
    return pl.pallas_call(
        _dispatch_kernel,
        grid_spec=pltpu.PrefetchScalarGridSpec(
            num_scalar_prefetch=2,
            grid=(n_prompt + 1,),
            in_specs=[pl.BlockSpec((1, 1, TOP_K * tm), lambda i, lb, tb: (i, 0, 0), memory_space=pltpu.SMEM),
                      pl.BlockSpec((tm, d), lambda i, lb, tb: (jnp.minimum(i, n_prompt - 1), 0)),
                      pl.BlockSpec((ts, d), lambda i, lb, tb: (0, 0))],
            out_specs=pl.BlockSpec(memory_space=pl.ANY),
            scratch_shapes=[pltpu.VMEM((MOE_BLOCK, d), h2_p.dtype), pltpu.SemaphoreType.DMA((2,))],
        ),
        out_shape=jax.ShapeDtypeStruct((n_buf, d), h2_p.dtype),
        compiler_params=_params("arbitrary"),
        name="dispatch",
    )(last_blk, tail_blk, tiles, h2_p, h2_s)


def _experts_kernel(blk_exp_ref, blk_rows_ref, x_ref, wgu_ref, bgu_ref, wdn_ref, bdn_ref, o_ref, wgu_b_ref, wdn_b_ref):
    p = pl.program_id(0)
    dff = wdn_ref.shape[1]
    chunk = 256

    @pl.when((p == 0) | (blk_exp_ref[p] != blk_exp_ref[jnp.maximum(p - 1, 0)]))
    def _():
        for c in range(2 * dff // chunk):
            wgu_b_ref[:, c * chunk:(c + 1) * chunk] = wgu_ref[0, :, c * chunk:(c + 1) * chunk].astype(BF16)
        for c in range(dff // chunk):
            wdn_b_ref[c * chunk:(c + 1) * chunk, :] = wdn_ref[0, c * chunk:(c + 1) * chunk, :].astype(BF16)

    n_rows = blk_rows_ref[p]

    @pl.when(n_rows > 0)
    def _():
        gu = jnp.dot(x_ref[...].astype(BF16), wgu_b_ref[...], preferred_element_type=F32) + bgu_ref[0]
        g = jnp.minimum(gu[:, :dff], SWIGLU_LIMIT)
        u = jnp.clip(gu[:, dff:], -SWIGLU_LIMIT, SWIGLU_LIMIT)
        act = (u + 1.0) * (g * jax.nn.sigmoid(g * SWIGLU_ALPHA))
        o_ref[...] = jnp.dot(act.astype(BF16), wdn_b_ref[...], preferred_element_type=F32) + bdn_ref[0]

    @pl.when(n_rows <= 0)
    def _():
        o_ref[...] = jnp.zeros(o_ref.shape, o_ref.dtype)


def _experts(xs, blk_exp, blk_rows, w_gu, b_gu, w_dn, b_dn):
    n_buf, d = xs.shape
    nblk = n_buf // MOE_BLOCK
    dff2 = w_gu.shape[2]
    return pl.pallas_call(
        _experts_kernel,
        grid_spec=pltpu.PrefetchScalarGridSpec(
            num_scalar_prefetch=2,
            grid=(nblk,),
            in_specs=[
                pl.BlockSpec((MOE_BLOCK, d), lambda p, be, br: (p, 0)),
                pl.BlockSpec((1, d, dff2), lambda p, be, br: (be[p], 0, 0)),
                pl.BlockSpec((1, 1, dff2), lambda p, be, br: (be[p], 0, 0)),
                pl.BlockSpec((1, dff2 // 2, d), lambda p, be, br: (be[p], 0, 0)),
                pl.BlockSpec((1, 1, d), lambda p, be, br: (be[p], 0, 0)),
            ],
            out_specs=pl.BlockSpec((MOE_BLOCK, d), lambda p, be, br: (p, 0)),
            scratch_shapes=[pltpu.VMEM((d, dff2), BF16), pltpu.VMEM((dff2 // 2, d), BF16)],
        ),
        out_shape=jax.ShapeDtypeStruct((n_buf, d), F32),
        compiler_params=_params("arbitrary"),
        name="experts",
    )(blk_exp, blk_rows, xs, w_gu, b_gu, w_dn, b_dn)


def _route(counts, n_blk):
    experts = jnp.arange(N_EXPERTS)
    padded = (counts + MOE_BLOCK - 1) // MOE_BLOCK * MOE_BLOCK
    ends = jnp.cumsum(padded)
    pstart = ends - padded
    blk_start = jnp.arange(n_blk) * MOE_BLOCK
    blk_exp = jnp.minimum(jnp.sum(ends[None, :] <= blk_start[:, None], axis=1), N_EXPERTS - 1)
    seg_end = jnp.sum(jnp.where(blk_exp[:, None] == experts[None, :], (pstart + counts)[None, :], 0), axis=1)
    blk_rows = jnp.clip(seg_end - blk_start, 0, MOE_BLOCK)
    last_blk = jnp.where(padded > 0, ends // MOE_BLOCK - 1, -1)
    tail_blk = ends[-1:] // MOE_BLOCK
    i32 = lambda a: a.astype(jnp.int32)
    return pstart, i32(blk_exp), i32(blk_rows), i32(last_blk), i32(tail_blk)


def _dest_rows(idx, rank, pstart):
    onehot = idx[..., None] == jnp.arange(N_EXPERTS)
    return (rank + jnp.sum(jnp.where(onehot, pstart, 0), axis=-1)).astype(jnp.int32)


def _final_rows(dest_ref, eo_ref, buf_ref, sem_ref, slot):
    def body(r8, _):
        r0 = pl.multiple_of(r8 * ROW_UNROLL, ROW_UNROLL)
        for k in range(ROW_UNROLL):
            pltpu.make_async_copy(eo_ref.at[pl.ds(dest_ref[0, 0, r0 + k], 1), :], buf_ref.at[slot, pl.ds(r0 + k, 1), :],
                                  sem_ref.at[slot]).start()
        return 0

    lax.fori_loop(0, buf_ref.shape[1] // ROW_UNROLL, body, 0)


def _final_kernel(dest_ref, dest_next_ref, x1_ref, gate_ref, gfin_ref, eo_ref, y_ref, buf_ref, sem_ref):
    i = pl.program_id(0)
    slot = i % 2
    tm = x1_ref.shape[0]

    @pl.when(i == 0)
    def _():
        _final_rows(dest_ref, eo_ref, buf_ref, sem_ref, slot)

    @pl.when(i + 1 < pl.num_programs(0))
    def _():
        _final_rows(dest_next_ref, eo_ref, buf_ref, sem_ref, 1 - slot)

    pltpu.make_async_copy(eo_ref.at[pl.ds(0, TOP_K * tm), :], buf_ref.at[slot], sem_ref.at[slot]).wait()
    gate = gate_ref[...]
    x2 = x1_ref[...]
    for j in range(TOP_K):
        x2 = x2 + gate[:, j:j + 1] * buf_ref[slot, j * tm:(j + 1) * tm, :]
    y_ref[...] = _rms(x2, gfin_ref[...])


def _final(x1, eo, dest, gate, g_final):
    t, d = x1.shape
    tm = min(FINAL_ROWS, t)
    n = t // tm
    tiles = _dest_tiles(dest, tm)
    idx_spec = lambda f: pl.BlockSpec((1, 1, TOP_K * tm), f, memory_space=pltpu.SMEM)
    return pl.pallas_call(
        _final_kernel,
        grid=(n,),
        in_specs=[idx_spec(lambda i: (i, 0, 0)),
                  idx_spec(lambda i: (jnp.minimum(i + 1, n - 1), 0, 0)),
                  pl.BlockSpec((tm, d), lambda i: (i, 0)),
                  pl.BlockSpec((tm, TOP_K), lambda i: (i, 0)),
                  pl.BlockSpec((1, d), lambda i: (0, 0)),
                  pl.BlockSpec(memory_space=pl.ANY)],
        out_specs=pl.BlockSpec((tm, d), lambda i: (i, 0)),
        out_shape=jax.ShapeDtypeStruct((t, d), F32),
        scratch_shapes=[pltpu.VMEM((2, TOP_K * tm, d), F32), pltpu.SemaphoreType.DMA((2,))],
        compiler_params=_params("arbitrary"),
        name="final",
    )(tiles, tiles, x1, gate, g_final, eo)


def _rope_tables(pos):
    inv = ROPE_THETA ** (-jnp.arange(0, HEAD_DIM, 2, dtype=F32) / HEAD_DIM)
    ang = pos.astype(F32)[:, None] * inv[None, :]
    cos, sin = jnp.cos(ang), jnp.sin(ang)
    return jnp.concatenate([cos, cos], axis=-1), jnp.concatenate([-sin, sin], axis=-1)


def kernel(x_prompt, x_sample, cache_k, cache_v, page_table, g_mix, w_in, ln_g, ln_b, w_s, b_s, w_br_attn, w_br_sgu, w_out, g_ffn, w_router, b_router, w_gu, b_gu, w_dn, b_dn, g_final):
    depth = w_in.shape[0]
    assert depth == 1, "the layer loop of this kernel is written for a single layer"
    layer = 0
    bsz, seq, d = x_prompt.shape
    db, dl, _ = x_sample.shape
    assert dl == 1
    n_pages = page_table.shape[1]
    past = n_pages * PAGE_SIZE
    ppb = MOBA_BLOCK // PAGE_SIZE
    assert past % MOBA_BLOCK == 0, "the sample group's own key block holds only the new row"
    nb_s = past // MOBA_BLOCK
    n_sel_s = min(MOBA_TOPK, nb_s)
    tp, ts = bsz * seq, db * dl

    row = lambda a: a[layer].reshape(1, -1)
    w_in_b = w_in[layer].astype(BF16)
    w_br_sgu_b = w_br_sgu[layer].astype(BF16)
    w_br_attn_b = w_br_attn[layer].astype(BF16)
    w_out_b = w_out[layer].astype(BF16)
    tril = jnp.tril(jnp.ones((SGU_CHUNK, SGU_CHUNK), F32))
    wsp_p = (w_s[layer] * tril).astype(BF16)
    bsp_p = jnp.repeat(b_s[layer].T, SGU_GROUP_DIM, axis=1)
    wsp_s = jnp.repeat(w_s[layer][:, 0, 0], SGU_GROUP_DIM).reshape(1, -1)
    bsp_s = jnp.repeat(b_s[layer][:, 0], SGU_GROUP_DIM).reshape(1, -1)

    cos_p, sin_p = _rope_tables(jnp.arange(seq))
    cos_s, sin_s = _rope_tables(jnp.repeat(past + jnp.arange(dl), db))

    proj = functools.partial(_proj, g_mix=row(g_mix), w_in_b=w_in_b, ln_g=row(ln_g), ln_b=row(ln_b),
                             w_br_sgu_b=w_br_sgu_b)
    q_p, k_p, v_p, kb_p, vb_p, kmean_p, sga_p, msgu_p = proj(
        x_prompt.reshape(tp, d), cos_p, sin_p, wsp=wsp_p, bsp=bsp_p, chunked=True, seq=seq)
    q_s, k_s, v_s, vn_s, sga_s, msgu_s = proj(
        x_sample.reshape(ts, d), cos_s, sin_s, wsp=wsp_s, bsp=bsp_s, chunked=False, seq=ts)

    a_p, pagesums = _moba_prompt(q_p.reshape(bsz, seq, d), kb_p.reshape(bsz, seq, d), vb_p.reshape(bsz, seq, d),
                                 kmean_p.reshape(bsz, seq // MOBA_BLOCK, d), cache_k[layer], page_table)

    heads = lambda a: a.reshape(db, N_HEADS, HEAD_DIM)
    sel = _blocksel(heads(q_s), pagesums.reshape(db, n_pages, N_HEADS, HEAD_DIM), n_sel_s, ppb)
    logical = sel.transpose(0, 2, 1)[..., None] * ppb + jnp.arange(ppb)
    hit = logical.reshape(db, N_HEADS, -1, 1) == jnp.arange(n_pages)
    phys = jnp.sum(jnp.where(hit, page_table[:, None, None, :], 0), axis=-1)
    a_s = _decode(phys.astype(jnp.int32), heads(q_s), heads(k_s), heads(v_s), cache_k[layer], cache_v[layer])

    post = functools.partial(_post, w_br_attn_b=w_br_attn_b, w_out_b=w_out_b, g_ffn=row(g_ffn),
                             w_router_t=w_router[layer].T, b_router_col=b_router[layer].reshape(-1, 1))
    x1_p, h2_p, idx_p, gate_p, rank_p, cnt_p = post(
        a_p.reshape(tp, d), sga_p, msgu_p, x_prompt.reshape(tp, d), jnp.zeros((N_EXPERTS, HEAD_DIM), F32))
    x1_s, h2_s, idx_s, gate_s, rank_s, cnt = post(a_s.reshape(ts, d), sga_s, msgu_s, x_sample.reshape(ts, d), cnt_p)

    n_blk = -(-((tp + ts) * TOP_K + N_EXPERTS * (MOE_BLOCK - 1)) // MOE_BLOCK)
    pstart, blk_exp, blk_rows, last_blk, tail_blk = _route(cnt[:, 0].astype(jnp.int32), n_blk)
    dest_p = _dest_rows(idx_p, rank_p, pstart)
    dest_s = _dest_rows(idx_s, rank_s, pstart)
    xs = _dispatch(h2_p, h2_s, dest_p, dest_s, last_blk, tail_blk, n_blk * MOE_BLOCK)
    eo = _experts(xs, blk_exp, blk_rows, w_gu[layer], b_gu[layer][:, None, :], w_dn[layer], b_dn[layer][:, None, :])
    y_p = _final(x1_p, eo, dest_p, gate_p.T, g_final.reshape(1, -1))
    y_s = _final(x1_s, eo, dest_s, gate_s.T, g_final.reshape(1, -1))

    return (y_p.reshape(bsz, seq, d), y_s.reshape(db, dl, d),
            k_p.reshape(depth, bsz, seq, N_HEADS, HEAD_DIM), v_p.reshape(depth, bsz, seq, N_HEADS, HEAD_DIM),
            k_s.reshape(depth, db, dl, N_HEADS, HEAD_DIM), v_s.reshape(depth, db, dl, N_HEADS, HEAD_DIM),
            vn_s.reshape(depth, db, dl, d))
```

```python
import functools

import jax
import jax.numpy as jnp
from jax import lax
from jax.experimental import pallas as pl
from jax.experimental.pallas import tpu as pltpu

N_HEADS = 8
HEAD_DIM = 128
MOBA_BLOCK = 256
MOBA_TOPK = 3
ROPE_THETA = 10000.0
PAGE_SIZE = 128
SGU_GROUPS = 8
SGU_GROUP_DIM = 128
SGU_CHUNK = 128
N_EXPERTS = 32
TOP_K = 4
MOE_BLOCK = 256
SWIGLU_LIMIT = 7.0
SWIGLU_ALPHA = 1.702
EPS = 1e-6
NEG = -1e30
N_PROJ = 7

F32 = jnp.float32
BF16 = jnp.bfloat16

VMEM_LIMIT_BYTES = 56 * 1024 * 1024
PROJ_ROWS = 256
POST_ROWS = 256
FINAL_ROWS = 256
MAX_PAGES_PER_STEP = 16
MOBA_GROUP = 4
ROW_UNROLL = 8


def _params(*sem):
    return pltpu.CompilerParams(dimension_semantics=sem, vmem_limit_bytes=VMEM_LIMIT_BYTES)


def _gelu(x):
    return 0.5 * x * (1.0 + lax.erf(x * 0.7071067811865476))


def _rms(x, g):
    return x * lax.rsqrt(jnp.mean(x * x, axis=-1, keepdims=True) + EPS) * g


def _proj_kernel(chunked, x_ref, cos_ref, sin_ref, gmix_ref, win_ref, lng_ref, lnb_ref,
                 wsp_ref, bsp_ref, wbs_ref, *out_refs):
    if chunked:
        q_ref, k_ref, v_ref, kb_ref, vb_ref, kmean_ref, sga_ref, msgu_ref = out_refs
    else:
        q_ref, k_ref, v_ref, vn_ref, sga_ref, msgu_ref = out_refs
    d = x_ref.shape[1]
    tm = x_ref.shape[0]
    hb = _rms(x_ref[...], gmix_ref[...]).astype(BF16)

    def proj(s):
        return jnp.dot(hb, win_ref[:, s * d:(s + 1) * d], preferred_element_type=F32)

    cos = cos_ref[...]
    sin = sin_ref[...]

    def rope(t, h):
        th = t[:, h * HEAD_DIM:(h + 1) * HEAD_DIM]
        return th * cos + pltpu.roll(th, HEAD_DIM // 2, axis=1) * sin

    q = proj(0)
    for h in range(N_HEADS):
        q_ref[:, h * HEAD_DIM:(h + 1) * HEAD_DIM] = rope(q, h)
    k = proj(1)
    for h in range(N_HEADS):
        kr = rope(k, h)
        cols = slice(h * HEAD_DIM, (h + 1) * HEAD_DIM)
        k_ref[:, cols] = kr
        if chunked:
            kb_ref[:, cols] = kr.astype(BF16)
            nb = tm // MOBA_BLOCK
            kmean_ref[:, 0, cols] = jnp.mean(kr.reshape(nb, MOBA_BLOCK, HEAD_DIM), axis=1)
    v = proj(2)
    v_ref[...] = v
    if chunked:
        vb_ref[...] = v.astype(BF16)

    gv = _gelu(proj(4))
    mu = jnp.mean(gv, axis=-1, keepdims=True)
    var = jnp.mean(jnp.square(gv - mu), axis=-1, keepdims=True)
    vn = (gv - mu) * lax.rsqrt(var + EPS) * lng_ref[...] + lnb_ref[...]
    gu = _gelu(proj(3))
    if chunked:
        vnb = vn.astype(BF16)
        for c in range(tm // SGU_CHUNK):
            rows = slice(c * SGU_CHUNK, (c + 1) * SGU_CHUNK)
            for g in range(SGU_GROUPS):
                cols = slice(g * SGU_GROUP_DIM, (g + 1) * SGU_GROUP_DIM)
                sp = jnp.dot(wsp_ref[g], vnb[rows, cols], preferred_element_type=F32) + bsp_ref[:, cols]
                msgu_ref[rows, cols] = gu[rows, cols] * sp
        sg = msgu_ref[...]
    else:
        vn_ref[...] = vn
        sg = gu * (wsp_ref[...] * vn + bsp_ref[...])
    sga_ref[...] = jax.nn.sigmoid(proj(5))
    sgb = jax.nn.sigmoid(proj(6))
    msgu_ref[...] = sgb * jnp.dot(sg.astype(BF16), wbs_ref[...], preferred_element_type=F32)


def _proj(x, cosf, sinf, g_mix, w_in_b, ln_g, ln_b, wsp, bsp, w_br_sgu_b, *, chunked, seq):
    t, d = x.shape
    tm = PROJ_ROWS if chunked else t
    assert t % tm == 0 and seq % tm == 0 and tm % MOBA_BLOCK == 0 or not chunked
    n_tab = seq // tm
    row = lambda i: (i, 0)
    full = lambda i: (0, 0)
    tab = lambda i: (i % n_tab, 0)
    in_specs = [
        pl.BlockSpec((tm, d), row),
        pl.BlockSpec((tm, HEAD_DIM), tab),
        pl.BlockSpec((tm, HEAD_DIM), tab),
        pl.BlockSpec((1, d), full),
        pl.BlockSpec((d, N_PROJ * d), full, pipeline_mode=pl.Buffered(1)),
        pl.BlockSpec((1, d), full),
        pl.BlockSpec((1, d), full),
        (pl.BlockSpec(wsp.shape, lambda i: (0, 0, 0)) if chunked else pl.BlockSpec((1, d), full)),
        pl.BlockSpec(bsp.shape, full),
        pl.BlockSpec((d, d), full, pipeline_mode=pl.Buffered(1)),
    ]
    f32_rows = jax.ShapeDtypeStruct((t, d), F32)
    bf_rows = jax.ShapeDtypeStruct((t, d), BF16)
    row_spec = pl.BlockSpec((tm, d), row)
    if chunked:
        nb = tm // MOBA_BLOCK
        out_shape = (f32_rows, f32_rows, f32_rows, bf_rows, bf_rows,
                     jax.ShapeDtypeStruct((t // MOBA_BLOCK, 1, d), F32), f32_rows, f32_rows)
        out_specs = (row_spec, row_spec, row_spec, row_spec, row_spec,
                     pl.BlockSpec((nb, 1, d), lambda i: (i, 0, 0)), row_spec, row_spec)
    else:
        out_shape = (f32_rows,) * 6
        out_specs = (row_spec,) * 6
    return pl.pallas_call(
        functools.partial(_proj_kernel, chunked),
        grid=(t // tm,),
        in_specs=in_specs,
        out_specs=out_specs,
        out_shape=out_shape,
        compiler_params=_params("parallel"),
        name="proj_prompt" if chunked else "proj_sample",
    )(x, cosf, sinf, g_mix, w_in_b, ln_g, ln_b, wsp, bsp, w_br_sgu_b)


def _topk_rows(score, valid, n_sel):
    nb = score.shape[0]
    blk = lax.broadcasted_iota(jnp.int32, score.shape, 0).astype(F32)
    work = jnp.where(valid, score, NEG)
    picked = jnp.zeros(score.shape, F32)
    for _ in range(n_sel):
        m = jnp.max(work, axis=0, keepdims=True)
        idx = jnp.min(jnp.where(work == m, blk, float(nb)), axis=0, keepdims=True)
        hit = blk == idx
        picked = jnp.where(hit, 1.0, picked)
        work = jnp.where(hit, -jnp.inf, work)
    return jnp.where(valid, picked, 0.0)


def _sum_rows(x):
    while x.shape[0] > 1:
        half = x.shape[0] // 2
        assert 2 * half == x.shape[0]
        x = x[:half] + x[half:]
    return x[0]


def _page_copies(cache_ref, pt_ref, buf_ref, sem_ref, chunk, slot):
    pps = buf_ref.shape[1]
    for j in range(pps):
        pltpu.make_async_copy(cache_ref.at[pt_ref[chunk * pps + j]], buf_ref.at[slot, j], sem_ref.at[slot]).start()


def _page_wait(cache_ref, buf_ref, sem_ref, slot):
    pltpu.make_async_copy(cache_ref.at[pl.ds(0, buf_ref.shape[1])], buf_ref.at[slot], sem_ref.at[slot]).wait()


def _tile_of_step(i, n):
    return jnp.where(i % 2 == 0, i // 2, n - 1 - i // 2)


def _moba_kernel(n_sel, group, pt_ref, q_ref, kb_ref, vb_ref, kmean_ref, cache_ref, o_ref, ps_ref, buf_ref, sem_ref):
    n_q = pl.num_programs(2)
    step = (pl.program_id(0) * pl.num_programs(1) + pl.program_id(1)) * n_q + pl.program_id(2)
    n_steps = pl.num_programs(0) * pl.num_programs(1) * n_q
    slot = step % 2

    @pl.when(step == 0)
    def _():
        _page_copies(cache_ref, pt_ref, buf_ref, sem_ref, step, slot)

    @pl.when(step + 1 < n_steps)
    def _():
        _page_copies(cache_ref, pt_ref, buf_ref, sem_ref, step + 1, 1 - slot)

    c = _tile_of_step(pl.program_id(2), n_q)
    tq = q_ref.shape[1]
    nb = kmean_ref.shape[1]
    q = q_ref[0]
    qs = (q * (HEAD_DIM ** -0.5 * 1.4426950408889634)).astype(BF16)

    bsc = lax.dot_general(kmean_ref[0], q, (((1,), (1,)), ((), ())),
                          precision=lax.Precision.HIGHEST, preferred_element_type=F32)
    valid = lax.broadcasted_iota(jnp.int32, (nb, tq), 0) < c
    bias_t = jnp.where(_topk_rows(bsc, valid, n_sel) > 0.5, 0.0, NEG)
    bias = jnp.concatenate([bias_t, jnp.zeros((HEAD_DIM - nb, tq), F32)], axis=0).T
    q_aug = jnp.concatenate([qs, bias.astype(BF16)], axis=1)

    def flash(s, vblk, carry):
        m, l, acc = carry
        m_new = jnp.maximum(m, jnp.max(s, axis=1, keepdims=True))
        alpha = jnp.exp2(m - m_new)
        p = jnp.exp2(s - m_new)
        l = alpha * l + jnp.sum(p, axis=1, keepdims=True)
        acc = alpha * acc + jnp.dot(p.astype(BF16), vblk, preferred_element_type=F32)
        return m_new, l, acc

    gk = group * MOBA_BLOCK
    key_blk = lax.broadcasted_iota(jnp.int32, (gk, HEAD_DIM), 0) // MOBA_BLOCK
    lane = lax.broadcasted_iota(jnp.int32, (gk, HEAD_DIM), 1)

    def past(gi, carry):
        rows = pl.ds(pl.multiple_of(gi * gk, gk), gk)
        onehot = jnp.where(lane == key_blk + gi * group, 1.0, 0.0).astype(BF16)
        k_aug = jnp.concatenate([kb_ref[0, rows, :], onehot], axis=1)
        s = lax.dot_general(q_aug, k_aug, (((1,), (1,)), ((), ())), preferred_element_type=F32)
        return flash(s, vb_ref[0, rows, :], carry)

    init = (jnp.full((tq, 1), -jnp.inf, F32), jnp.zeros((tq, 1), F32), jnp.zeros((tq, HEAD_DIM), F32))
    carry = lax.fori_loop(0, (c + group - 1) // group, past, init)

    rows = pl.ds(pl.multiple_of(c * MOBA_BLOCK, MOBA_BLOCK), MOBA_BLOCK)
    s = lax.dot_general(qs, kb_ref[0, rows, :], (((1,), (1,)), ((), ())), preferred_element_type=F32)
    qpos = lax.broadcasted_iota(jnp.int32, (tq, MOBA_BLOCK), 0)
    kpos = lax.broadcasted_iota(jnp.int32, (tq, MOBA_BLOCK), 1)
    s = jnp.where(kpos <= qpos, s, NEG)
    _, l, acc = flash(s, vb_ref[0, rows, :], carry)
    o_ref[0] = (acc / l).astype(o_ref.dtype)

    _page_wait(cache_ref, buf_ref, sem_ref, slot)
    for j in range(buf_ref.shape[1]):
        ps_ref[j] = _sum_rows(buf_ref[slot, j])


def _moba_prompt(q, kb, vb, kmean, cache_k_layer, page_table):
    b, s, w = q.shape
    nb = kmean.shape[1]
    assert s % MOBA_BLOCK == 0 and nb == s // MOBA_BLOCK and nb <= HEAD_DIM
    tq = MOBA_BLOCK
    n_q = s // tq
    group = max(g for g in (1, 2, 4) if g <= MOBA_GROUP and nb % g == 0)
    _, page, nh, hd = cache_k_layer.shape
    total = page_table.shape[0] * page_table.shape[1]
    n_steps = b * N_HEADS * n_q
    pps = -(-total // n_steps)
    assert pps <= MAX_PAGES_PER_STEP, "page buffers for this many pages per step do not fit VMEM"
    pages = page_table.reshape(-1)
    pages = jnp.concatenate([pages, jnp.broadcast_to(pages[-1:], (n_steps * pps - total,))])
    tile = lambda bi, h, i, pt: (bi, _tile_of_step(i, n_q), h)
    whole = lambda bi, h, i, pt: (bi, 0, h)
    att, sums = pl.pallas_call(
        functools.partial(_moba_kernel, min(MOBA_TOPK, nb), group),
        grid_spec=pltpu.PrefetchScalarGridSpec(
            num_scalar_prefetch=1,
            grid=(b, N_HEADS, n_q),
            in_specs=[
                pl.BlockSpec((1, tq, HEAD_DIM), tile),
                pl.BlockSpec((1, s, HEAD_DIM), whole),
                pl.BlockSpec((1, s, HEAD_DIM), whole),
                pl.BlockSpec((1, nb, HEAD_DIM), whole),
                pl.BlockSpec(memory_space=pl.ANY),
            ],
            out_specs=(pl.BlockSpec((1, tq, HEAD_DIM), tile),
                       pl.BlockSpec((pps, nh, hd), lambda bi, h, i, pt: ((bi * N_HEADS + h) * n_q + i, 0, 0))),
            scratch_shapes=[pltpu.VMEM((2, pps, page, nh, hd), F32), pltpu.SemaphoreType.DMA((2,))],
        ),
        out_shape=(jax.ShapeDtypeStruct((b, s, w), BF16), jax.ShapeDtypeStruct((n_steps * pps, nh, hd), F32)),
        compiler_params=_params("arbitrary", "arbitrary", "arbitrary"),
        name="moba_prompt",
    )(pages, q, kb, vb, kmean, cache_k_layer)
    return att, sums[:total]


def _blocksel_kernel(n_sel, ppb, q_ref, ps_ref, sel_ref):
    nblk = ps_ref.shape[1] // ppb
    ps = ps_ref[0]
    ps = ps.reshape(nblk, ppb, N_HEADS, HEAD_DIM)
    kmean = jnp.sum(ps, axis=1) / MOBA_BLOCK
    bsc = jnp.sum(kmean * q_ref[0][None], axis=-1)
    blk = lax.broadcasted_iota(jnp.int32, bsc.shape, 0)
    out = []
    for _ in range(n_sel):
        m = jnp.max(bsc, axis=0, keepdims=True)
        idx = jnp.min(jnp.where(bsc == m, blk, nblk), axis=0, keepdims=True)
        out.append(idx)
        bsc = jnp.where(blk == idx, -jnp.inf, bsc)
    sel_ref[0] = jnp.concatenate(out, axis=0)


def _blocksel(q_s, pagesums, n_sel, ppb):
    db, n_pages = pagesums.shape[:2]
    return pl.pallas_call(
        functools.partial(_blocksel_kernel, n_sel, ppb),
        grid=(db,),
        in_specs=[
            pl.BlockSpec((1, N_HEADS, HEAD_DIM), lambda n: (n, 0, 0)),
            pl.BlockSpec((1, n_pages, N_HEADS, HEAD_DIM), lambda n: (n, 0, 0, 0)),
        ],
        out_specs=pl.BlockSpec((1, n_sel, N_HEADS), lambda n: (n, 0, 0)),
        out_shape=jax.ShapeDtypeStruct((db, n_sel, N_HEADS), jnp.int32),
        compiler_params=_params("parallel"),
        name="blocksel",
    )(q_s, pagesums)


def _decode_dma(phys_ref, ck_ref, cv_ref, kbuf_ref, vbuf_ref, sem_ref, n, slot):
    n_pp = phys_ref.shape[2]
    copies = []
    for h in range(N_HEADS):
        for j in range(n_pp):
            page = phys_ref[n, h, j]
            rows = pl.ds(j * PAGE_SIZE, PAGE_SIZE)
            copies.append(pltpu.make_async_copy(ck_ref.at[page, :, h, :], kbuf_ref.at[slot, h, rows, :], sem_ref.at[0, slot]))
            copies.append(pltpu.make_async_copy(cv_ref.at[page, :, h, :], vbuf_ref.at[slot, h, rows, :], sem_ref.at[1, slot]))
    return copies


def _decode_kernel(phys_ref, q_ref, k_ref, v_ref, ck_ref, cv_ref, o_ref, kbuf_ref, vbuf_ref, sem_ref):
    n = pl.program_id(0)
    slot = n % 2
    scale = HEAD_DIM ** -0.5

    @pl.when(n == 0)
    def _():
        for cp in _decode_dma(phys_ref, ck_ref, cv_ref, kbuf_ref, vbuf_ref, sem_ref, n, slot):
            cp.start()

    @pl.when(n + 1 < pl.num_programs(0))
    def _():
        for cp in _decode_dma(phys_ref, ck_ref, cv_ref, kbuf_ref, vbuf_ref, sem_ref, n + 1, 1 - slot):
            cp.start()

    for cp in _decode_dma(phys_ref, ck_ref, cv_ref, kbuf_ref, vbuf_ref, sem_ref, n, slot):
        cp.wait()

    q = q_ref[0]
    s_new = jnp.sum(q * k_ref[0], axis=-1, keepdims=True) * scale
    for h in range(N_HEADS):
        qh = q[h:h + 1, :]
        s = jnp.sum(kbuf_ref[slot, h] * qh, axis=-1, keepdims=True) * scale
        sn = s_new[h:h + 1, :]
        m = jnp.maximum(jnp.max(s, axis=0, keepdims=True), sn)
        p = jnp.exp(s - m)
        pn = jnp.exp(sn - m)
        l = jnp.sum(p, axis=0, keepdims=True) + pn
        acc = jnp.sum(p * vbuf_ref[slot, h], axis=0, keepdims=True) + pn * v_ref[0, h:h + 1, :]
        o_ref[0, h:h + 1, :] = acc / l


def _decode(phys, q_s, k_s, v_s, cache_k_layer, cache_v_layer):
    db, _, n_pp = phys.shape
    nk = n_pp * PAGE_SIZE
    row = pl.BlockSpec((1, N_HEADS, HEAD_DIM), lambda n, ph: (n, 0, 0))
    return pl.pallas_call(
        _decode_kernel,
        grid_spec=pltpu.PrefetchScalarGridSpec(
            num_scalar_prefetch=1,
            grid=(db,),
            in_specs=[row, row, row, pl.BlockSpec(memory_space=pl.ANY), pl.BlockSpec(memory_space=pl.ANY)],
            out_specs=row,
            scratch_shapes=[
                pltpu.VMEM((2, N_HEADS, nk, HEAD_DIM), F32),
                pltpu.VMEM((2, N_HEADS, nk, HEAD_DIM), F32),
                pltpu.SemaphoreType.DMA((2, 2)),
            ],
        ),
        out_shape=jax.ShapeDtypeStruct((db, N_HEADS, HEAD_DIM), F32),
        compiler_params=_params("arbitrary"),
        name="decode_attn",
    )(phys, q_s, k_s, v_s, cache_k_layer, cache_v_layer)


def _post_kernel(a_ref, sga_ref, msgu_ref, x_ref, wba_ref, wo_ref, gffn_ref, wrt_ref, br_ref, cnt_in_ref,
                 x1_ref, h2_ref, idx_ref, gate_ref, rank_ref, cnt_ref):
    tm = x_ref.shape[0]

    @pl.when(pl.program_id(0) == 0)
    def _():
        cnt_ref[...] = cnt_in_ref[...]

    merged = sga_ref[...] * jnp.dot(a_ref[...].astype(BF16), wba_ref[...], preferred_element_type=F32) + msgu_ref[...]
    x1 = x_ref[...] + jnp.dot(merged.astype(BF16), wo_ref[...], preferred_element_type=F32)
    x1_ref[...] = x1
    h2 = _rms(x1, gffn_ref[...])
    h2_ref[...] = h2
    logits = lax.dot_general(wrt_ref[...], h2, (((1,), (1,)), ((), ())),
                             precision=lax.Precision.HIGHEST, preferred_element_type=F32) + br_ref[...]
    exp_id = lax.broadcasted_iota(jnp.int32, logits.shape, 0).astype(F32)
    vals, hits = [], []
    for _ in range(TOP_K):
        m = jnp.max(logits, axis=0, keepdims=True)
        idx = jnp.min(jnp.where(logits == m, exp_id, float(N_EXPERTS)), axis=0, keepdims=True)
        hit = exp_id == idx
        vals.append(m)
        hits.append((idx, hit))
        logits = jnp.where(hit, -jnp.inf, logits)
    ex = [jnp.exp(v - vals[0]) for v in vals]
    den = ex[0] + ex[1] + ex[2] + ex[3]
    gate_ref[...] = jnp.concatenate([e / den for e in ex], axis=0)
    idx_ref[...] = jnp.concatenate([i for i, _ in hits], axis=0).astype(jnp.int32)

    sel = jnp.zeros(logits.shape, F32)
    for _, hit in hits:
        sel = jnp.where(hit, 1.0, sel)
    earlier = (lax.broadcasted_iota(jnp.int32, (tm, tm), 0) < lax.broadcasted_iota(jnp.int32, (tm, tm), 1))
    before = jnp.dot(sel.astype(BF16), jnp.where(earlier, 1.0, 0.0).astype(BF16), preferred_element_type=F32)
    before = before + cnt_ref[:, 0:1]
    rank_ref[...] = jnp.concatenate(
        [jnp.sum(jnp.where(hit, before, 0.0), axis=0, keepdims=True) for _, hit in hits], axis=0).astype(jnp.int32)
    cnt_ref[...] = cnt_ref[...] + jnp.sum(sel, axis=1, keepdims=True)


def _post(a, sga, msgu, x, counts_in, w_br_attn_b, w_out_b, g_ffn, w_router_t, b_router_col):
    t, d = x.shape
    tm = min(POST_ROWS, t)
    assert t % tm == 0
    row = pl.BlockSpec((tm, d), lambda i: (i, 0))
    full = lambda i: (0, 0)
    tok = pl.BlockSpec((TOP_K, tm), lambda i: (0, i))
    cnt = pl.BlockSpec((N_EXPERTS, HEAD_DIM), full)
    return pl.pallas_call(
        _post_kernel,
        grid=(t // tm,),
        in_specs=[row, row, row, row,
                  pl.BlockSpec((d, d), full), pl.BlockSpec((d, d), full), pl.BlockSpec((1, d), full),
                  pl.BlockSpec((N_EXPERTS, d), full), pl.BlockSpec((N_EXPERTS, 1), full), cnt],
        out_specs=(row, row, tok, tok, tok, cnt),
        out_shape=(jax.ShapeDtypeStruct((t, d), F32), jax.ShapeDtypeStruct((t, d), F32),
                   jax.ShapeDtypeStruct((TOP_K, t), jnp.int32), jax.ShapeDtypeStruct((TOP_K, t), F32),
                   jax.ShapeDtypeStruct((TOP_K, t), jnp.int32), jax.ShapeDtypeStruct((N_EXPERTS, HEAD_DIM), F32)),
        compiler_params=_params("arbitrary"),
        name="post",
    )(a, sga, msgu, x, w_br_attn_b, w_out_b, g_ffn, w_router_t, b_router_col, counts_in)


def _dest_tiles(dest, tm):
    t = dest.shape[1]
    return dest.reshape(TOP_K, t // tm, tm).transpose(1, 0, 2).reshape(t // tm, 1, TOP_K * tm)


def _dispatch_kernel(last_blk_ref, tail_ref, dest_ref, hp_ref, hs_ref, xs_ref, zeros_ref, sem_ref):
    i = pl.program_id(0)
    n_prompt = pl.num_programs(0) - 1

    @pl.when(i == 0)
    def _():
        zeros_ref[...] = jnp.zeros(zeros_ref.shape, zeros_ref.dtype)

        def fill(blk):
            rows = pl.ds(pl.multiple_of(blk * MOE_BLOCK, MOE_BLOCK), MOE_BLOCK)
            return pltpu.make_async_copy(zeros_ref, xs_ref.at[rows, :], sem_ref.at[1])

        for start in (True, False):
            act = (lambda cp: cp.start()) if start else (lambda cp: cp.wait())
            for e in range(N_EXPERTS):
                @pl.when(last_blk_ref[e] >= 0)
                def _():
                    act(fill(last_blk_ref[e]))

            def tail(blk, _):
                act(fill(blk))
                return 0

            lax.fori_loop(tail_ref[0], xs_ref.shape[0] // MOE_BLOCK, tail, 0)

    def copy_rows(h_ref):
        tm = h_ref.shape[0]

        def body(r8, _):
            r0 = pl.multiple_of(r8 * ROW_UNROLL, ROW_UNROLL)
            for k in range(ROW_UNROLL):
                for j in range(TOP_K):
                    pltpu.make_async_copy(h_ref.at[pl.ds(r0 + k, 1), :],
                                          xs_ref.at[pl.ds(dest_ref[0, 0, j * tm + r0 + k], 1), :], sem_ref.at[0]).start()
            return 0

        lax.fori_loop(0, tm // ROW_UNROLL, body, 0)
        for _ in range(TOP_K):
            pltpu.make_async_copy(h_ref, xs_ref.at[pl.ds(0, tm), :], sem_ref.at[0]).wait()

    @pl.when(i < n_prompt)
    def _():
        copy_rows(hp_ref)

    @pl.when(i == n_prompt)
    def _():
        copy_rows(hs_ref)


def _dispatch(h2_p, h2_s, dest_p, dest_s, last_blk, tail_blk, n_buf):
    tp, d = h2_p.shape
    ts = h2_s.shape[0]
    tm = POST_ROWS
    assert tp % tm == 0 and ts <= tm and ts % ROW_UNROLL == 0
    n_prompt = tp // tm
    tile_s = jnp.pad(_dest_tiles(dest_s, ts), ((0, 0), (0, 0), (0, TOP_K * (tm - ts))))
    tiles = jnp.concatenate([_dest_tiles(dest_p, tm), tile_s], axis=0)
    return pl.pallas_call(
        _dispatch_kernel,
        grid_spec=pltpu.PrefetchScalarGridSpec(
            num_scalar_prefetch=2,
            grid=(n_prompt + 1,),
            in_specs=[pl.BlockSpec((1, 1, TOP_K * tm), lambda i, lb, tb: (i, 0, 0), memory_space=pltpu.SMEM),
                      pl.BlockSpec((tm, d), lambda i, lb, tb: (jnp.minimum(i, n_prompt - 1), 0)),
                      pl.BlockSpec((ts, d), lambda i, lb, tb: (0, 0))],
            out_specs=pl.BlockSpec(memory_space=pl.ANY),
            scratch_shapes=[pltpu.VMEM((MOE_BLOCK, d), h2_p.dtype), pltpu.SemaphoreType.DMA((2,))],
        ),
        out_shape=jax.ShapeDtypeStruct((n_buf, d), h2_p.dtype),
        compiler_params=_params("arbitrary"),
        name="dispatch",
    )(last_blk, tail_blk, tiles, h2_p, h2_s)


def _experts_kernel(blk_exp_ref, blk_rows_ref, x_ref, wgu_ref, bgu_ref, wdn_ref, bdn_ref, o_ref, wgu_b_ref, wdn_b_ref):
    p = pl.program_id(0)
    dff = wdn_ref.shape[1]
    chunk = 256

    @pl.when((p == 0) | (blk_exp_ref[p] != blk_exp_ref[jnp.maximum(p - 1, 0)]))
    def _():
        for c in range(2 * dff // chunk):
            wgu_b_ref[:, c * chunk:(c + 1) * chunk] = wgu_ref[0, :, c * chunk:(c + 1) * chunk].astype(BF16)
        for c in range(dff // chunk):
            wdn_b_ref[c * chunk:(c + 1) * chunk, :] = wdn_ref[0, c * chunk:(c + 1) * chunk, :].astype(BF16)

    n_rows = blk_rows_ref[p]

    @pl.when(n_rows > 0)
    def _():
        gu = jnp.dot(x_ref[...].astype(BF16), wgu_b_ref[...], preferred_element_type=F32) + bgu_ref[0]
        g = jnp.minimum(gu[:, :dff], SWIGLU_LIMIT)
        u = jnp.clip(gu[:, dff:], -SWIGLU_LIMIT, SWIGLU_LIMIT)
        act = (u + 1.0) * (g * jax.nn.sigmoid(g * SWIGLU_ALPHA))
        o_ref[...] = jnp.dot(act.astype(BF16), wdn_b_ref[...], preferred_element_type=F32) + bdn_ref[0]

    @pl.when(n_rows <= 0)
    def _():
        o_ref[...] = jnp.zeros(o_ref.shape, o_ref.dtype)


def _experts(xs, blk_exp, blk_rows, w_gu, b_gu, w_dn, b_dn):
    n_buf, d = xs.shape
    nblk = n_buf // MOE_BLOCK
    dff2 = w_gu.shape[2]
    return pl.pallas_call(
        _experts_kernel,
        grid_spec=pltpu.PrefetchScalarGridSpec(
            num_scalar_prefetch=2,
            grid=(nblk,),
            in_specs=[
                pl.BlockSpec((MOE_BLOCK, d), lambda p, be, br: (p, 0)),
                pl.BlockSpec((1, d, dff2), lambda p, be, br: (be[p], 0, 0)),
                pl.BlockSpec((1, 1, dff2), lambda p, be, br: (be[p], 0, 0)),
                pl.BlockSpec((1, dff2 // 2, d), lambda p, be, br: (be[p], 0, 0)),
                pl.BlockSpec((1, 1, d), lambda p, be, br: (be[p], 0, 0)),
            ],
            out_specs=pl.BlockSpec((MOE_BLOCK, d), lambda p, be, br: (p, 0)),
            scratch_shapes=[pltpu.VMEM((d, dff2), BF16), pltpu.VMEM((dff2 // 2, d), BF16)],
        ),
        out_shape=jax.ShapeDtypeStruct((n_buf, d), F32),
        compiler_params=_params("arbitrary"),
        name="experts",
    )(blk_exp, blk_rows, xs, w_gu, b_gu, w_dn, b_dn)


def _route(counts, n_blk):
    experts = jnp.arange(N_EXPERTS)
    padded = (counts + MOE_BLOCK - 1) // MOE_BLOCK * MOE_BLOCK
    ends = jnp.cumsum(padded)
    pstart = ends - padded
    blk_start = jnp.arange(n_blk) * MOE_BLOCK
    blk_exp = jnp.minimum(jnp.sum(ends[None, :] <= blk_start[:, None], axis=1), N_EXPERTS - 1)
    seg_end = jnp.sum(jnp.where(blk_exp[:, None] == experts[None, :], (pstart + counts)[None, :], 0), axis=1)
    blk_rows = jnp.clip(seg_end - blk_start, 0, MOE_BLOCK)
    last_blk = jnp.where(padded > 0, ends // MOE_BLOCK - 1, -1)
    tail_blk = ends[-1:] // MOE_BLOCK
    i32 = lambda a: a.astype(jnp.int32)
    return pstart, i32(blk_exp), i32(blk_rows), i32(last_blk), i32(tail_blk)


def _dest_rows(idx, rank, pstart):
    onehot = idx[..., None] == jnp.arange(N_EXPERTS)
    return (rank + jnp.sum(jnp.where(onehot, pstart, 0), axis=-1)).astype(jnp.int32)


def _final_rows(dest_ref, eo_ref, buf_ref, sem_ref, slot):
    tm = buf_ref.shape[2]

    def body(r8, _):
        r0 = pl.multiple_of(r8 * ROW_UNROLL, ROW_UNROLL)
        for k in range(ROW_UNROLL):
            for j in range(TOP_K):
                pltpu.make_async_copy(eo_ref.at[pl.ds(dest_ref[0, 0, j * tm + r0 + k], 1), :],
                                      buf_ref.at[slot, j, pl.ds(r0 + k, 1), :], sem_ref.at[slot]).start()
        return 0

    lax.fori_loop(0, tm // ROW_UNROLL, body, 0)


def _final_kernel(dest_ref, dest_next_ref, x1_ref, gate_ref, gfin_ref, eo_ref, y_ref, buf_ref, sem_ref):
    i = pl.program_id(0)
    slot = i % 2
    tm = x1_ref.shape[0]

    @pl.when(i == 0)
    def _():
        _final_rows(dest_ref, eo_ref, buf_ref, sem_ref, slot)

    @pl.when(i + 1 < pl.num_programs(0))
    def _():
        _final_rows(dest_next_ref, eo_ref, buf_ref, sem_ref, 1 - slot)

    for j in range(TOP_K):
        pltpu.make_async_copy(eo_ref.at[pl.ds(0, tm), :], buf_ref.at[slot, j], sem_ref.at[slot]).wait()
    gate = gate_ref[...]
    x2 = x1_ref[...]
    for j in range(TOP_K):
        x2 = x2 + gate[:, j:j + 1] * buf_ref[slot, j]
    y_ref[...] = _rms(x2, gfin_ref[...])


def _final(x1, eo, dest, gate, g_final):
    t, d = x1.shape
    tm = min(FINAL_ROWS, t)
    n = t // tm
    tiles = _dest_tiles(dest, tm)
    idx_spec = lambda f: pl.BlockSpec((1, 1, TOP_K * tm), f, memory_space=pltpu.SMEM)
    return pl.pallas_call(
        _final_kernel,
        grid=(n,),
        in_specs=[idx_spec(lambda i: (i, 0, 0)),
                  idx_spec(lambda i: (jnp.minimum(i + 1, n - 1), 0, 0)),
                  pl.BlockSpec((tm, d), lambda i: (i, 0)),
                  pl.BlockSpec((tm, TOP_K), lambda i: (i, 0)),
                  pl.BlockSpec((1, d), lambda i: (0, 0)),
                  pl.BlockSpec(memory_space=pl.ANY)],
        out_specs=pl.BlockSpec((tm, d), lambda i: (i, 0)),
        out_shape=jax.ShapeDtypeStruct((t, d), F32),
        scratch_shapes=[pltpu.VMEM((2, TOP_K, tm, d), F32), pltpu.SemaphoreType.DMA((2,))],
        compiler_params=_params("arbitrary"),
        name="final",
    )(tiles, tiles, x1, gate, g_final, eo)


def _rope_tables(pos):
    inv = ROPE_THETA ** (-jnp.arange(0, HEAD_DIM, 2, dtype=F32) / HEAD_DIM)
    ang = pos.astype(F32)[:, None] * inv[None, :]
    cos, sin = jnp.cos(ang), jnp.sin(ang)
    return jnp.concatenate([cos, cos], axis=-1), jnp.concatenate([-sin, sin], axis=-1)


def kernel(x_prompt, x_sample, cache_k, cache_v, page_table, g_mix, w_in, ln_g, ln_b, w_s, b_s, w_br_attn, w_br_sgu, w_out, g_ffn, w_router, b_router, w_gu, b_gu, w_dn, b_dn, g_final):
    depth = w_in.shape[0]
    assert depth == 1, "the layer loop of this kernel is written for a single layer"
    layer = 0
    bsz, seq, d = x_prompt.shape
    db, dl, _ = x_sample.shape
    assert dl == 1
    n_pages = page_table.shape[1]
    past = n_pages * PAGE_SIZE
    ppb = MOBA_BLOCK // PAGE_SIZE
    assert past % MOBA_BLOCK == 0, "the sample group's own key block holds only the new row"
    nb_s = past // MOBA_BLOCK
    n_sel_s = min(MOBA_TOPK, nb_s)
    tp, ts = bsz * seq, db * dl

    row = lambda a: a[layer].reshape(1, -1)
    w_in_b = w_in[layer].astype(BF16)
    w_br_sgu_b = w_br_sgu[layer].astype(BF16)
    w_br_attn_b = w_br_attn[layer].astype(BF16)
    w_out_b = w_out[layer].astype(BF16)
    tril = jnp.tril(jnp.ones((SGU_CHUNK, SGU_CHUNK), F32))
    wsp_p = (w_s[layer] * tril).astype(BF16)
    bsp_p = jnp.repeat(b_s[layer].T, SGU_GROUP_DIM, axis=1)
    wsp_s = jnp.repeat(w_s[layer][:, 0, 0], SGU_GROUP_DIM).reshape(1, -1)
    bsp_s = jnp.repeat(b_s[layer][:, 0], SGU_GROUP_DIM).reshape(1, -1)

    cos_p, sin_p = _rope_tables(jnp.arange(seq))
    cos_s, sin_s = _rope_tables(jnp.repeat(past + jnp.arange(dl), db))

    proj = functools.partial(_proj, g_mix=row(g_mix), w_in_b=w_in_b, ln_g=row(ln_g), ln_b=row(ln_b),
                             w_br_sgu_b=w_br_sgu_b)
    q_p, k_p, v_p, kb_p, vb_p, kmean_p, sga_p, msgu_p = proj(
        x_prompt.reshape(tp, d), cos_p, sin_p, wsp=wsp_p, bsp=bsp_p, chunked=True, seq=seq)
    q_s, k_s, v_s, vn_s, sga_s, msgu_s = proj(
        x_sample.reshape(ts, d), cos_s, sin_s, wsp=wsp_s, bsp=bsp_s, chunked=False, seq=ts)

    a_p, pagesums = _moba_prompt(q_p.reshape(bsz, seq, d), kb_p.reshape(bsz, seq, d), vb_p.reshape(bsz, seq, d),
                                 kmean_p.reshape(bsz, seq // MOBA_BLOCK, d), cache_k[layer], page_table)

    heads = lambda a: a.reshape(db, N_HEADS, HEAD_DIM)
    sel = _blocksel(heads(q_s), pagesums.reshape(db, n_pages, N_HEADS, HEAD_DIM), n_sel_s, ppb)
    logical = sel.transpose(0, 2, 1)[..., None] * ppb + jnp.arange(ppb)
    hit = logical.reshape(db, N_HEADS, -1, 1) == jnp.arange(n_pages)
    phys = jnp.sum(jnp.where(hit, page_table[:, None, None, :], 0), axis=-1)
    a_s = _decode(phys.astype(jnp.int32), heads(q_s), heads(k_s), heads(v_s), cache_k[layer], cache_v[layer])

    post = functools.partial(_post, w_br_attn_b=w_br_attn_b, w_out_b=w_out_b, g_ffn=row(g_ffn),
                             w_router_t=w_router[layer].T, b_router_col=b_router[layer].reshape(-1, 1))
    x1_p, h2_p, idx_p, gate_p, rank_p, cnt_p = post(
        a_p.reshape(tp, d), sga_p, msgu_p, x_prompt.reshape(tp, d), jnp.zeros((N_EXPERTS, HEAD_DIM), F32))
    x1_s, h2_s, idx_s, gate_s, rank_s, cnt = post(a_s.reshape(ts, d), sga_s, msgu_s, x_sample.reshape(ts, d), cnt_p)

    n_blk = -(-((tp + ts) * TOP_K + N_EXPERTS * (MOE_BLOCK - 1)) // MOE_BLOCK)
    pstart, blk_exp, blk_rows, last_blk, tail_blk = _route(cnt[:, 0].astype(jnp.int32), n_blk)
    dest_p = _dest_rows(idx_p, rank_p, pstart)
    dest_s = _dest_rows(idx_s, rank_s, pstart)
    xs = _dispatch(h2_p, h2_s, dest_p, dest_s, last_blk, tail_blk, n_blk * MOE_BLOCK)
    eo = _experts(xs, blk_exp, blk_rows, w_gu[layer], b_gu[layer][:, None, :], w_dn[layer], b_dn[layer][:, None, :])
    y_p = _final(x1_p, eo, dest_p, gate_p.T, g_final.reshape(1, -1))
    y_s = _final(x1_s, eo, dest_s, gate_s.T, g_final.reshape(1, -1))

    return (y_p.reshape(bsz, seq, d), y_s.reshape(db, dl, d),
            k_p.reshape(depth, bsz, seq, N_HEADS, HEAD_DIM), v_p.reshape(depth, bsz, seq, N_HEADS, HEAD_DIM),
            k_s.reshape(depth, db, dl, N_HEADS, HEAD_DIM), v_s.reshape(depth, db, dl, N_HEADS, HEAD_DIM),
            vn_s.reshape(depth, db, dl, d))
```

```python
import functools

import jax
import jax.numpy as jnp
from jax import lax
from jax.experimental import pallas as pl
from jax.experimental.pallas import tpu as pltpu

N_HEADS = 8
HEAD_DIM = 128
MOBA_BLOCK = 256
MOBA_TOPK = 3
ROPE_THETA = 10000.0
PAGE_SIZE = 128
SGU_GROUPS = 8
SGU_GROUP_DIM = 128
SGU_CHUNK = 128
N_EXPERTS = 32
TOP_K = 4
MOE_BLOCK = 256
SWIGLU_LIMIT = 7.0
SWIGLU_ALPHA = 1.702
EPS = 1e-6
NEG = -1e30
N_PROJ = 7

F32 = jnp.float32
BF16 = jnp.bfloat16

VMEM_LIMIT_BYTES = 56 * 1024 * 1024
PROJ_ROWS = 256
POST_ROWS = 256
FINAL_ROWS = 256
MAX_PAGES_PER_STEP = 16
PAGE_BUFFERS = 3
MOBA_GROUP = 4
ROW_UNROLL = 8


def _params(*sem):
    return pltpu.CompilerParams(dimension_semantics=sem, vmem_limit_bytes=VMEM_LIMIT_BYTES)


def _gelu(x):
    return 0.5 * x * (1.0 + lax.erf(x * 0.7071067811865476))


def _rms(x, g):
    return x * lax.rsqrt(jnp.mean(x * x, axis=-1, keepdims=True) + EPS) * g


def _proj_kernel(chunked, x_ref, cos_ref, sin_ref, gmix_ref, win_ref, lng_ref, lnb_ref,
                 wsp_ref, bsp_ref, wbs_ref, *out_refs):
    if chunked:
        q_ref, k_ref, v_ref, kb_ref, vb_ref, kmean_ref, sga_ref, msgu_ref = out_refs
    else:
        q_ref, k_ref, v_ref, vn_ref, sga_ref, msgu_ref = out_refs
    d = x_ref.shape[1]
    tm = x_ref.shape[0]
    hb = _rms(x_ref[...], gmix_ref[...]).astype(BF16)

    def proj(s):
        return jnp.dot(hb, win_ref[:, s * d:(s + 1) * d], preferred_element_type=F32)

    cos = cos_ref[...]
    sin = sin_ref[...]

    def rope(t, h):
        th = t[:, h * HEAD_DIM:(h + 1) * HEAD_DIM]
        return th * cos + pltpu.roll(th, HEAD_DIM // 2, axis=1) * sin

    q = proj(0)
    for h in range(N_HEADS):
        q_ref[:, h * HEAD_DIM:(h + 1) * HEAD_DIM] = rope(q, h)
    k = proj(1)
    for h in range(N_HEADS):
        kr = rope(k, h)
        cols = slice(h * HEAD_DIM, (h + 1) * HEAD_DIM)
        k_ref[:, cols] = kr
        if chunked:
            kb_ref[:, cols] = kr.astype(BF16)
            nb = tm // MOBA_BLOCK
            kmean_ref[:, 0, cols] = jnp.mean(kr.reshape(nb, MOBA_BLOCK, HEAD_DIM), axis=1)
    v = proj(2)
    v_ref[...] = v
    if chunked:
        vb_ref[...] = v.astype(BF16)

    gv = _gelu(proj(4))
    mu = jnp.mean(gv, axis=-1, keepdims=True)
    var = jnp.mean(jnp.square(gv - mu), axis=-1, keepdims=True)
    vn = (gv - mu) * lax.rsqrt(var + EPS) * lng_ref[...] + lnb_ref[...]
    gu = _gelu(proj(3))
    if chunked:
        vnb = vn.astype(BF16)
        for c in range(tm // SGU_CHUNK):
            rows = slice(c * SGU_CHUNK, (c + 1) * SGU_CHUNK)
            for g in range(SGU_GROUPS):
                cols = slice(g * SGU_GROUP_DIM, (g + 1) * SGU_GROUP_DIM)
                sp = jnp.dot(wsp_ref[g], vnb[rows, cols], preferred_element_type=F32) + bsp_ref[:, cols]
                msgu_ref[rows, cols] = gu[rows, cols] * sp
        sg = msgu_ref[...]
    else:
        vn_ref[...] = vn
        sg = gu * (wsp_ref[...] * vn + bsp_ref[...])
    sga_ref[...] = jax.nn.sigmoid(proj(5))
    sgb = jax.nn.sigmoid(proj(6))
    msgu_ref[...] = sgb * jnp.dot(sg.astype(BF16), wbs_ref[...], preferred_element_type=F32)


def _proj(x, cosf, sinf, g_mix, w_in_b, ln_g, ln_b, wsp, bsp, w_br_sgu_b, *, chunked, seq):
    t, d = x.shape
    tm = PROJ_ROWS if chunked else t
    assert t % tm == 0 and seq % tm == 0 and tm % MOBA_BLOCK == 0 or not chunked
    n_tab = seq // tm
    row = lambda i: (i, 0)
    full = lambda i: (0, 0)
    tab = lambda i: (i % n_tab, 0)
    in_specs = [
        pl.BlockSpec((tm, d), row),
        pl.BlockSpec((tm, HEAD_DIM), tab),
        pl.BlockSpec((tm, HEAD_DIM), tab),
        pl.BlockSpec((1, d), full),
        pl.BlockSpec((d, N_PROJ * d), full, pipeline_mode=pl.Buffered(1)),
        pl.BlockSpec((1, d), full),
        pl.BlockSpec((1, d), full),
        (pl.BlockSpec(wsp.shape, lambda i: (0, 0, 0)) if chunked else pl.BlockSpec((1, d), full)),
        pl.BlockSpec(bsp.shape, full),
        pl.BlockSpec((d, d), full, pipeline_mode=pl.Buffered(1)),
    ]
    f32_rows = jax.ShapeDtypeStruct((t, d), F32)
    bf_rows = jax.ShapeDtypeStruct((t, d), BF16)
    row_spec = pl.BlockSpec((tm, d), row)
    if chunked:
        nb = tm // MOBA_BLOCK
        out_shape = (f32_rows, f32_rows, f32_rows, bf_rows, bf_rows,
                     jax.ShapeDtypeStruct((t // MOBA_BLOCK, 1, d), F32), f32_rows, f32_rows)
        out_specs = (row_spec, row_spec, row_spec, row_spec, row_spec,
                     pl.BlockSpec((nb, 1, d), lambda i: (i, 0, 0)), row_spec, row_spec)
    else:
        out_shape = (f32_rows,) * 6
        out_specs = (row_spec,) * 6
    return pl.pallas_call(
        functools.partial(_proj_kernel, chunked),
        grid=(t // tm,),
        in_specs=in_specs,
        out_specs=out_specs,
        out_shape=out_shape,
        compiler_params=_params("parallel"),
        name="proj_prompt" if chunked else "proj_sample",
    )(x, cosf, sinf, g_mix, w_in_b, ln_g, ln_b, wsp, bsp, w_br_sgu_b)


def _topk_rows(score, valid, n_sel):
    nb = score.shape[0]
    blk = lax.broadcasted_iota(jnp.int32, score.shape, 0).astype(F32)
    work = jnp.where(valid, score, NEG)
    picked = jnp.zeros(score.shape, F32)
    for _ in range(n_sel):
        m = jnp.max(work, axis=0, keepdims=True)
        idx = jnp.min(jnp.where(work == m, blk, float(nb)), axis=0, keepdims=True)
        hit = blk == idx
        picked = jnp.where(hit, 1.0, picked)
        work = jnp.where(hit, -jnp.inf, work)
    return jnp.where(valid, picked, 0.0)


def _sum_rows(x):
    while x.shape[0] > 1:
        half = x.shape[0] // 2
        assert 2 * half == x.shape[0]
        x = x[:half] + x[half:]
    return x[0]


def _page_copies(cache_ref, pt_ref, buf_ref, sem_ref, chunk, slot):
    pps = buf_ref.shape[1]
    for j in range(pps):
        pltpu.make_async_copy(cache_ref.at[pt_ref[chunk * pps + j]], buf_ref.at[slot, j], sem_ref.at[slot]).start()


def _page_wait(cache_ref, buf_ref, sem_ref, slot):
    pltpu.make_async_copy(cache_ref.at[pl.ds(0, buf_ref.shape[1])], buf_ref.at[slot], sem_ref.at[slot]).wait()


def _tile_of_step(i, n):
    return jnp.where(i % 2 == 0, i // 2, n - 1 - i // 2)


def _moba_kernel(n_sel, group, pt_ref, q_ref, kb_ref, vb_ref, kmean_ref, cache_ref, o_ref, ps_ref, buf_ref, sem_ref):
    n_q = pl.num_programs(2)
    step = (pl.program_id(0) * pl.num_programs(1) + pl.program_id(1)) * n_q + pl.program_id(2)
    n_steps = pl.num_programs(0) * pl.num_programs(1) * n_q
    depth = buf_ref.shape[0]
    ahead = depth - 1
    slot = step % depth

    @pl.when(step == 0)
    def _():
        for g in range(ahead):
            @pl.when(g < n_steps)
            def _():
                _page_copies(cache_ref, pt_ref, buf_ref, sem_ref, g, g)

    @pl.when(step + ahead < n_steps)
    def _():
        _page_copies(cache_ref, pt_ref, buf_ref, sem_ref, step + ahead, (step + ahead) % depth)

    c = _tile_of_step(pl.program_id(2), n_q)
    tq = q_ref.shape[1]
    nb = kmean_ref.shape[1]
    q = q_ref[0]
    qs = (q * (HEAD_DIM ** -0.5 * 1.4426950408889634)).astype(BF16)

    bsc = lax.dot_general(kmean_ref[0], q, (((1,), (1,)), ((), ())),
                          precision=lax.Precision.HIGHEST, preferred_element_type=F32)
    valid = lax.broadcasted_iota(jnp.int32, (nb, tq), 0) < c
    bias_t = jnp.where(_topk_rows(bsc, valid, n_sel) > 0.5, 0.0, NEG)
    bias = jnp.concatenate([bias_t, jnp.zeros((HEAD_DIM - nb, tq), F32)], axis=0).T
    q_aug = jnp.concatenate([qs, bias.astype(BF16)], axis=1)

    def flash(s, vblk, carry):
        m, l, acc = carry
        m_new = jnp.maximum(m, jnp.max(s, axis=1, keepdims=True))
        alpha = jnp.exp2(m - m_new)
        p = jnp.exp2(s - m_new)
        l = alpha * l + jnp.sum(p, axis=1, keepdims=True)
        acc = alpha * acc + jnp.dot(p.astype(BF16), vblk, preferred_element_type=F32)
        return m_new, l, acc

    gk = group * MOBA_BLOCK
    key_blk = lax.broadcasted_iota(jnp.int32, (gk, HEAD_DIM), 0) // MOBA_BLOCK
    lane = lax.broadcasted_iota(jnp.int32, (gk, HEAD_DIM), 1)

    def past(gi, carry):
        rows = pl.ds(pl.multiple_of(gi * gk, gk), gk)
        onehot = jnp.where(lane == key_blk + gi * group, 1.0, 0.0).astype(BF16)
        k_aug = jnp.concatenate([kb_ref[0, rows, :], onehot], axis=1)
        s = lax.dot_general(q_aug, k_aug, (((1,), (1,)), ((), ())), preferred_element_type=F32)
        return flash(s, vb_ref[0, rows, :], carry)

    init = (jnp.full((tq, 1), -jnp.inf, F32), jnp.zeros((tq, 1), F32), jnp.zeros((tq, HEAD_DIM), F32))
    carry = lax.fori_loop(0, (c + group - 1) // group, past, init)

    rows = pl.ds(pl.multiple_of(c * MOBA_BLOCK, MOBA_BLOCK), MOBA_BLOCK)
    s = lax.dot_general(qs, kb_ref[0, rows, :], (((1,), (1,)), ((), ())), preferred_element_type=F32)
    qpos = lax.broadcasted_iota(jnp.int32, (tq, MOBA_BLOCK), 0)
    kpos = lax.broadcasted_iota(jnp.int32, (tq, MOBA_BLOCK), 1)
    s = jnp.where(kpos <= qpos, s, NEG)
    _, l, acc = flash(s, vb_ref[0, rows, :], carry)
    o_ref[0] = (acc / l).astype(o_ref.dtype)

    _page_wait(cache_ref, buf_ref, sem_ref, slot)
    for j in range(buf_ref.shape[1]):
        ps_ref[j] = _sum_rows(buf_ref[slot, j])


def _moba_prompt(q, kb, vb, kmean, cache_k_layer, page_table):
    b, s, w = q.shape
    nb = kmean.shape[1]
    assert s % MOBA_BLOCK == 0 and nb == s // MOBA_BLOCK and nb <= HEAD_DIM
    tq = MOBA_BLOCK
    n_q = s // tq
    group = max(g for g in (1, 2, 4) if g <= MOBA_GROUP and nb % g == 0)
    _, page, nh, hd = cache_k_layer.shape
    total = page_table.shape[0] * page_table.shape[1]
    n_steps = b * N_HEADS * n_q
    pps = -(-total // n_steps)
    assert pps <= MAX_PAGES_PER_STEP, "page buffers for this many pages per step do not fit VMEM"
    pages = page_table.reshape(-1)
    pages = jnp.concatenate([pages, jnp.broadcast_to(pages[-1:], (n_steps * pps - total,))])
    tile = lambda bi, h, i, pt: (bi, _tile_of_step(i, n_q), h)
    whole = lambda bi, h, i, pt: (bi, 0, h)
    att, sums = pl.pallas_call(
        functools.partial(_moba_kernel, min(MOBA_TOPK, nb), group),
        grid_spec=pltpu.PrefetchScalarGridSpec(
            num_scalar_prefetch=1,
            grid=(b, N_HEADS, n_q),
            in_specs=[
                pl.BlockSpec((1, tq, HEAD_DIM), tile),
                pl.BlockSpec((1, s, HEAD_DIM), whole),
                pl.BlockSpec((1, s, HEAD_DIM), whole),
                pl.BlockSpec((1, nb, HEAD_DIM), whole),
                pl.BlockSpec(memory_space=pl.ANY),
            ],
            out_specs=(pl.BlockSpec((1, tq, HEAD_DIM), tile),
                       pl.BlockSpec((pps, nh, hd), lambda bi, h, i, pt: ((bi * N_HEADS + h) * n_q + i, 0, 0))),
            scratch_shapes=[pltpu.VMEM((PAGE_BUFFERS, pps, page, nh, hd), F32),
                            pltpu.SemaphoreType.DMA((PAGE_BUFFERS,))],
        ),
        out_shape=(jax.ShapeDtypeStruct((b, s, w), BF16), jax.ShapeDtypeStruct((n_steps * pps, nh, hd), F32)),
        compiler_params=_params("arbitrary", "arbitrary", "arbitrary"),
        name="moba_prompt",
    )(pages, q, kb, vb, kmean, cache_k_layer)
    return att, sums[:total]


def _blocksel_kernel(n_sel, ppb, q_ref, ps_ref, sel_ref):
    nblk = ps_ref.shape[1] // ppb
    ps = ps_ref[0]
    ps = ps.reshape(nblk, ppb, N_HEADS, HEAD_DIM)
    kmean = jnp.sum(ps, axis=1) / MOBA_BLOCK
    bsc = jnp.sum(kmean * q_ref[0][None], axis=-1)
    blk = lax.broadcasted_iota(jnp.int32, bsc.shape, 0)
    out = []
    for _ in range(n_sel):
        m = jnp.max(bsc, axis=0, keepdims=True)
        idx = jnp.min(jnp.where(bsc == m, blk, nblk), axis=0, keepdims=True)
        out.append(idx)
        bsc = jnp.where(blk == idx, -jnp.inf, bsc)
    sel_ref[0] = jnp.concatenate(out, axis=0)


def _blocksel(q_s, pagesums, n_sel, ppb):
    db, n_pages = pagesums.shape[:2]
    return pl.pallas_call(
        functools.partial(_blocksel_kernel, n_sel, ppb),
        grid=(db,),
        in_specs=[
            pl.BlockSpec((1, N_HEADS, HEAD_DIM), lambda n: (n, 0, 0)),
            pl.BlockSpec((1, n_pages, N_HEADS, HEAD_DIM), lambda n: (n, 0, 0, 0)),
        ],
        out_specs=pl.BlockSpec((1, n_sel, N_HEADS), lambda n: (n, 0, 0)),
        out_shape=jax.ShapeDtypeStruct((db, n_sel, N_HEADS), jnp.int32),
        compiler_params=_params("parallel"),
        name="blocksel",
    )(q_s, pagesums)


def _decode_dma(phys_ref, ck_ref, cv_ref, kbuf_ref, vbuf_ref, sem_ref, n, slot):
    n_pp = phys_ref.shape[2]
    copies = []
    for h in range(N_HEADS):
        for j in range(n_pp):
            page = phys_ref[n, h, j]
            rows = pl.ds(j * PAGE_SIZE, PAGE_SIZE)
            copies.append(pltpu.make_async_copy(ck_ref.at[page, :, h, :], kbuf_ref.at[slot, h, rows, :], sem_ref.at[0, slot]))
            copies.append(pltpu.make_async_copy(cv_ref.at[page, :, h, :], vbuf_ref.at[slot, h, rows, :], sem_ref.at[1, slot]))
    return copies


def _decode_kernel(phys_ref, q_ref, k_ref, v_ref, ck_ref, cv_ref, o_ref, kbuf_ref, vbuf_ref, sem_ref):
    n = pl.program_id(0)
    slot = n % 2
    scale = HEAD_DIM ** -0.5

    @pl.when(n == 0)
    def _():
        for cp in _decode_dma(phys_ref, ck_ref, cv_ref, kbuf_ref, vbuf_ref, sem_ref, n, slot):
            cp.start()

    @pl.when(n + 1 < pl.num_programs(0))
    def _():
        for cp in _decode_dma(phys_ref, ck_ref, cv_ref, kbuf_ref, vbuf_ref, sem_ref, n + 1, 1 - slot):
            cp.start()

    for cp in _decode_dma(phys_ref, ck_ref, cv_ref, kbuf_ref, vbuf_ref, sem_ref, n, slot):
        cp.wait()

    q = q_ref[0]
    s_new = jnp.sum(q * k_ref[0], axis=-1, keepdims=True) * scale
    for h in range(N_HEADS):
        qh = q[h:h + 1, :]
        s = jnp.sum(kbuf_ref[slot, h] * qh, axis=-1, keepdims=True) * scale
        sn = s_new[h:h + 1, :]
        m = jnp.maximum(jnp.max(s, axis=0, keepdims=True), sn)
        p = jnp.exp(s - m)
        pn = jnp.exp(sn - m)
        l = jnp.sum(p, axis=0, keepdims=True) + pn
        acc = jnp.sum(p * vbuf_ref[slot, h], axis=0, keepdims=True) + pn * v_ref[0, h:h + 1, :]
        o_ref[0, h:h + 1, :] = acc / l


def _decode(phys, q_s, k_s, v_s, cache_k_layer, cache_v_layer):
    db, _, n_pp = phys.shape
    nk = n_pp * PAGE_SIZE
    row = pl.BlockSpec((1, N_HEADS, HEAD_DIM), lambda n, ph: (n, 0, 0))
    return pl.pallas_call(
        _decode_kernel,
        grid_spec=pltpu.PrefetchScalarGridSpec(
            num_scalar_prefetch=1,
            grid=(db,),
            in_specs=[row, row, row, pl.BlockSpec(memory_space=pl.ANY), pl.BlockSpec(memory_space=pl.ANY)],
            out_specs=row,
            scratch_shapes=[
                pltpu.VMEM((2, N_HEADS, nk, HEAD_DIM), F32),
                pltpu.VMEM((2, N_HEADS, nk, HEAD_DIM), F32),
                pltpu.SemaphoreType.DMA((2, 2)),
            ],
        ),
        out_shape=jax.ShapeDtypeStruct((db, N_HEADS, HEAD_DIM), F32),
        compiler_params=_params("arbitrary"),
        name="decode_attn",
    )(phys, q_s, k_s, v_s, cache_k_layer, cache_v_layer)


def _post_kernel(a_ref, sga_ref, msgu_ref, x_ref, wba_ref, wo_ref, gffn_ref, wrt_ref, br_ref, cnt_in_ref,
                 x1_ref, h2_ref, idx_ref, gate_ref, rank_ref, cnt_ref):
    tm = x_ref.shape[0]

    @pl.when(pl.program_id(0) == 0)
    def _():
        cnt_ref[...] = cnt_in_ref[...]

    merged = sga_ref[...] * jnp.dot(a_ref[...].astype(BF16), wba_ref[...], preferred_element_type=F32) + msgu_ref[...]
    x1 = x_ref[...] + jnp.dot(merged.astype(BF16), wo_ref[...], preferred_element_type=F32)
    x1_ref[...] = x1
    h2 = _rms(x1, gffn_ref[...])
    h2_ref[...] = h2
    logits = lax.dot_general(wrt_ref[...], h2, (((1,), (1,)), ((), ())),
                             precision=lax.Precision.HIGHEST, preferred_element_type=F32) + br_ref[...]
    exp_id = lax.broadcasted_iota(jnp.int32, logits.shape, 0).astype(F32)
    vals, hits = [], []
    for _ in range(TOP_K):
        m = jnp.max(logits, axis=0, keepdims=True)
        idx = jnp.min(jnp.where(logits == m, exp_id, float(N_EXPERTS)), axis=0, keepdims=True)
        hit = exp_id == idx
        vals.append(m)
        hits.append((idx, hit))
        logits = jnp.where(hit, -jnp.inf, logits)
    ex = [jnp.exp(v - vals[0]) for v in vals]
    den = ex[0] + ex[1] + ex[2] + ex[3]
    gate_ref[...] = jnp.concatenate([e / den for e in ex], axis=0)
    idx_ref[...] = jnp.concatenate([i for i, _ in hits], axis=0).astype(jnp.int32)

    sel = jnp.zeros(logits.shape, F32)
    for _, hit in hits:
        sel = jnp.where(hit, 1.0, sel)
    earlier = (lax.broadcasted_iota(jnp.int32, (tm, tm), 0) < lax.broadcasted_iota(jnp.int32, (tm, tm), 1))
    before = jnp.dot(sel.astype(BF16), jnp.where(earlier, 1.0, 0.0).astype(BF16), preferred_element_type=F32)
    before = before + cnt_ref[:, 0:1]
    rank_ref[...] = jnp.concatenate(
        [jnp.sum(jnp.where(hit, before, 0.0), axis=0, keepdims=True) for _, hit in hits], axis=0).astype(jnp.int32)
    cnt_ref[...] = cnt_ref[...] + jnp.sum(sel, axis=1, keepdims=True)


def _post(a, sga, msgu, x, counts_in, w_br_attn_b, w_out_b, g_ffn, w_router_t, b_router_col):
    t, d = x.shape
    tm = min(POST_ROWS, t)
    assert t % tm == 0
    row = pl.BlockSpec((tm, d), lambda i: (i, 0))
    full = lambda i: (0, 0)
    tok = pl.BlockSpec((TOP_K, tm), lambda i: (0, i))
    cnt = pl.BlockSpec((N_EXPERTS, HEAD_DIM), full)
    return pl.pallas_call(
        _post_kernel,
        grid=(t // tm,),
        in_specs=[row, row, row, row,
                  pl.BlockSpec((d, d), full), pl.BlockSpec((d, d), full), pl.BlockSpec((1, d), full),
                  pl.BlockSpec((N_EXPERTS, d), full), pl.BlockSpec((N_EXPERTS, 1), full), cnt],
        out_specs=(row, row, tok, tok, tok, cnt),
        out_shape=(jax.ShapeDtypeStruct((t, d), F32), jax.ShapeDtypeStruct((t, d), F32),
                   jax.ShapeDtypeStruct((TOP_K, t), jnp.int32), jax.ShapeDtypeStruct((TOP_K, t), F32),
                   jax.ShapeDtypeStruct((TOP_K, t), jnp.int32), jax.ShapeDtypeStruct((N_EXPERTS, HEAD_DIM), F32)),
        compiler_params=_params("arbitrary"),
        name="post",
    )(a, sga, msgu, x, w_br_attn_b, w_out_b, g_ffn, w_router_t, b_router_col, counts_in)


def _dest_tiles(dest, tm):
    t = dest.shape[1]
    return dest.reshape(TOP_K, t // tm, tm).transpose(1, 0, 2).reshape(t // tm, 1, TOP_K * tm)


def _dispatch_kernel(last_blk_ref, tail_ref, dest_ref, hp_ref, hs_ref, xs_ref, zeros_ref, sem_ref):
    i = pl.program_id(0)
    n_prompt = pl.num_programs(0) - 1

    @pl.when(i == 0)
    def _():
        zeros_ref[...] = jnp.zeros(zeros_ref.shape, zeros_ref.dtype)

        def fill(blk):
            rows = pl.ds(pl.multiple_of(blk * MOE_BLOCK, MOE_BLOCK), MOE_BLOCK)
            return pltpu.make_async_copy(zeros_ref, xs_ref.at[rows, :], sem_ref.at[1])

        for start in (True, False):
            act = (lambda cp: cp.start()) if start else (lambda cp: cp.wait())
            for e in range(N_EXPERTS):
                @pl.when(last_blk_ref[e] >= 0)
                def _():
                    act(fill(last_blk_ref[e]))

            def tail(blk, _):
                act(fill(blk))
                return 0

            lax.fori_loop(tail_ref[0], xs_ref.shape[0] // MOE_BLOCK, tail, 0)

    def copy_rows(h_ref):
        tm = h_ref.shape[0]

        def body(r8, _):
            r0 = pl.multiple_of(r8 * ROW_UNROLL, ROW_UNROLL)
            for k in range(ROW_UNROLL):
                for j in range(TOP_K):
                    pltpu.make_async_copy(h_ref.at[pl.ds(r0 + k, 1), :],
                                          xs_ref.at[pl.ds(dest_ref[0, 0, j * tm + r0 + k], 1), :], sem_ref.at[0]).start()
            return 0

        lax.fori_loop(0, tm // ROW_UNROLL, body, 0)
        for _ in range(TOP_K):
            pltpu.make_async_copy(h_ref, xs_ref.at[pl.ds(0, tm), :], sem_ref.at[0]).wait()

    @pl.when(i < n_prompt)
    def _():
        copy_rows(hp_ref)

    @pl.when(i == n_prompt)
    def _():
        copy_rows(hs_ref)


def _dispatch(h2_p, h2_s, dest_p, dest_s, last_blk, tail_blk, n_buf):
    tp, d = h2_p.shape
    ts = h2_s.shape[0]
    tm = POST_ROWS
    assert tp % tm == 0 and ts <= tm and ts % ROW_UNROLL == 0
    n_prompt = tp // tm
    tile_s = jnp.pad(_dest_tiles(dest_s, ts), ((0, 0), (0, 0), (0, TOP_K * (tm - ts))))
    tiles = jnp.concatenate([_dest_tiles(dest_p, tm), tile_s], axis=0)
    return pl.pallas_call(
        _dispatch_kernel,
        grid_spec=pltpu.PrefetchScalarGridSpec(
            num_scalar_prefetch=2,
            grid=(n_prompt + 1,),
            in_specs=[pl.BlockSpec((1, 1, TOP_K * tm), lambda i, lb, tb: (i, 0, 0), memory_space=pltpu.SMEM),
                      pl.BlockSpec((tm, d), lambda i, lb, tb: (jnp.minimum(i, n_prompt - 1), 0)),
                      pl.BlockSpec((ts, d), lambda i, lb, tb: (0, 0))],
            out_specs=pl.BlockSpec(memory_space=pl.ANY),
            scratch_shapes=[pltpu.VMEM((MOE_BLOCK, d), h2_p.dtype), pltpu.SemaphoreType.DMA((2,))],
        ),
        out_shape=jax.ShapeDtypeStruct((n_buf, d), h2_p.dtype),
        compiler_params=_params("arbitrary"),
        name="dispatch",
    )(last_blk, tail_blk, tiles, h2_p, h2_s)


def _experts_kernel(blk_exp_ref, blk_rows_ref, x_ref, wgu_ref, bgu_ref, wdn_ref, bdn_ref, o_ref, wgu_b_ref, wdn_b_ref):
    p = pl.program_id(0)
    dff = wdn_ref.shape[1]
    chunk = 256

    @pl.when((p == 0) | (blk_exp_ref[p] != blk_exp_ref[jnp.maximum(p - 1, 0)]))
    def _():
        for c in range(2 * dff // chunk):
            wgu_b_ref[:, c * chunk:(c + 1) * chunk] = wgu_ref[0, :, c * chunk:(c + 1) * chunk].astype(BF16)
        for c in range(dff // chunk):
            wdn_b_ref[c * chunk:(c + 1) * chunk, :] = wdn_ref[0, c * chunk:(c + 1) * chunk, :].astype(BF16)

    n_rows = blk_rows_ref[p]

    @pl.when(n_rows > 0)
    def _():
        gu = jnp.dot(x_ref[...].astype(BF16), wgu_b_ref[...], preferred_element_type=F32) + bgu_ref[0]
        g = jnp.minimum(gu[:, :dff], SWIGLU_LIMIT)
        u = jnp.clip(gu[:, dff:], -SWIGLU_LIMIT, SWIGLU_LIMIT)
        act = (u + 1.0) * (g * jax.nn.sigmoid(g * SWIGLU_ALPHA))
        o_ref[...] = jnp.dot(act.astype(BF16), wdn_b_ref[...], preferred_element_type=F32) + bdn_ref[0]

    @pl.when(n_rows <= 0)
    def _():
        o_ref[...] = jnp.zeros(o_ref.shape, o_ref.dtype)


def _experts(xs, blk_exp, blk_rows, w_gu, b_gu, w_dn, b_dn):
    n_buf, d = xs.shape
    nblk = n_buf // MOE_BLOCK
    dff2 = w_gu.shape[2]
    return pl.pallas_call(
        _experts_kernel,
        grid_spec=pltpu.PrefetchScalarGridSpec(
            num_scalar_prefetch=2,
            grid=(nblk,),
            in_specs=[
                pl.BlockSpec((MOE_BLOCK, d), lambda p, be, br: (p, 0)),
                pl.BlockSpec((1, d, dff2), lambda p, be, br: (be[p], 0, 0)),
                pl.BlockSpec((1, 1, dff2), lambda p, be, br: (be[p], 0, 0)),
                pl.BlockSpec((1, dff2 // 2, d), lambda p, be, br: (be[p], 0, 0)),
                pl.BlockSpec((1, 1, d), lambda p, be, br: (be[p], 0, 0)),
            ],
            out_specs=pl.BlockSpec((MOE_BLOCK, d), lambda p, be, br: (p, 0)),
            scratch_shapes=[pltpu.VMEM((d, dff2), BF16), pltpu.VMEM((dff2 // 2, d), BF16)],
        ),
        out_shape=jax.ShapeDtypeStruct((n_buf, d), F32),
        compiler_params=_params("arbitrary"),
        name="experts",
    )(blk_exp, blk_rows, xs, w_gu, b_gu, w_dn, b_dn)


def _route(counts, n_blk):
    experts = jnp.arange(N_EXPERTS)
    padded = (counts + MOE_BLOCK - 1) // MOE_BLOCK * MOE_BLOCK
    ends = jnp.cumsum(padded)
    pstart = ends - padded
    blk_start = jnp.arange(n_blk) * MOE_BLOCK
    blk_exp = jnp.minimum(jnp.sum(ends[None, :] <= blk_start[:, None], axis=1), N_EXPERTS - 1)
    seg_end = jnp.sum(jnp.where(blk_exp[:, None] == experts[None, :], (pstart + counts)[None, :], 0), axis=1)
    blk_rows = jnp.clip(seg_end - blk_start, 0, MOE_BLOCK)
    last_blk = jnp.where(padded > 0, ends // MOE_BLOCK - 1, -1)
    tail_blk = ends[-1:] // MOE_BLOCK
    i32 = lambda a: a.astype(jnp.int32)
    return pstart, i32(blk_exp), i32(blk_rows), i32(last_blk), i32(tail_blk)


def _dest_rows(idx, rank, pstart):
    onehot = idx[..., None] == jnp.arange(N_EXPERTS)
    return (rank + jnp.sum(jnp.where(onehot, pstart, 0), axis=-1)).astype(jnp.int32)


def _final_rows(dest_ref, eo_ref, buf_ref, sem_ref, slot):
    tm = buf_ref.shape[2]

    def body(r8, _):
        r0 = pl.multiple_of(r8 * ROW_UNROLL, ROW_UNROLL)
        for k in range(ROW_UNROLL):
            for j in range(TOP_K):
                pltpu.make_async_copy(eo_ref.at[pl.ds(dest_ref[0, 0, j * tm + r0 + k], 1), :],
                                      buf_ref.at[slot, j, pl.ds(r0 + k, 1), :], sem_ref.at[slot]).start()
        return 0

    lax.fori_loop(0, tm // ROW_UNROLL, body, 0)


def _final_kernel(dest_ref, dest_next_ref, x1_ref, gate_ref, gfin_ref, eo_ref, y_ref, buf_ref, sem_ref):
    i = pl.program_id(0)
    slot = i % 2
    tm = x1_ref.shape[0]

    @pl.when(i == 0)
    def _():
        _final_rows(dest_ref, eo_ref, buf_ref, sem_ref, slot)

    @pl.when(i + 1 < pl.num_programs(0))
    def _():
        _final_rows(dest_next_ref, eo_ref, buf_ref, sem_ref, 1 - slot)

    for j in range(TOP_K):
        pltpu.make_async_copy(eo_ref.at[pl.ds(0, tm), :], buf_ref.at[slot, j], sem_ref.at[slot]).wait()
    gate = gate_ref[...]
    x2 = x1_ref[...]
    for j in range(TOP_K):
        x2 = x2 + gate[:, j:j + 1] * buf_ref[slot, j]
    y_ref[...] = _rms(x2, gfin_ref[...])


def _final(x1, eo, dest, gate, g_final):
    t, d = x1.shape
    tm = min(FINAL_ROWS, t)
    n = t // tm
    tiles = _dest_tiles(dest, tm)
    idx_spec = lambda f: pl.BlockSpec((1, 1, TOP_K * tm), f, memory_space=pltpu.SMEM)
    return pl.pallas_call(
        _final_kernel,
        grid=(n,),
        in_specs=[idx_spec(lambda i: (i, 0, 0)),
                  idx_spec(lambda i: (jnp.minimum(i + 1, n - 1), 0, 0)),
                  pl.BlockSpec((tm, d), lambda i: (i, 0)),
                  pl.BlockSpec((tm, TOP_K), lambda i: (i, 0)),
                  pl.BlockSpec((1, d), lambda i: (0, 0)),
                  pl.BlockSpec(memory_space=pl.ANY)],
        out_specs=pl.BlockSpec((tm, d), lambda i: (i, 0)),
        out_shape=jax.ShapeDtypeStruct((t, d), F32),
        scratch_shapes=[pltpu.VMEM((2, TOP_K, tm, d), F32), pltpu.SemaphoreType.DMA((2,))],
        compiler_params=_params("arbitrary"),
        name="final",
    )(tiles, tiles, x1, gate, g_final, eo)


def _rope_tables(pos):
    inv = ROPE_THETA ** (-jnp.arange(0, HEAD_DIM, 2, dtype=F32) / HEAD_DIM)
    ang = pos.astype(F32)[:, None] * inv[None, :]
    cos, sin = jnp.cos(ang), jnp.sin(ang)
    return jnp.concatenate([cos, cos], axis=-1), jnp.concatenate([-sin, sin], axis=-1)


def kernel(x_prompt, x_sample, cache_k, cache_v, page_table, g_mix, w_in, ln_g, ln_b, w_s, b_s, w_br_attn, w_br_sgu, w_out, g_ffn, w_router, b_router, w_gu, b_gu, w_dn, b_dn, g_final):
    depth = w_in.shape[0]
    assert depth == 1, "the layer loop of this kernel is written for a single layer"
    layer = 0
    bsz, seq, d = x_prompt.shape
    db, dl, _ = x_sample.shape
    assert dl == 1
    n_pages = page_table.shape[1]
    past = n_pages * PAGE_SIZE
    ppb = MOBA_BLOCK // PAGE_SIZE
    assert past % MOBA_BLOCK == 0, "the sample group's own key block holds only the new row"
    nb_s = past // MOBA_BLOCK
    n_sel_s = min(MOBA_TOPK, nb_s)
    tp, ts = bsz * seq, db * dl

    row = lambda a: a[layer].reshape(1, -1)
    w_in_b = w_in[layer].astype(BF16)
    w_br_sgu_b = w_br_sgu[layer].astype(BF16)
    w_br_attn_b = w_br_attn[layer].astype(BF16)
    w_out_b = w_out[layer].astype(BF16)
    tril = jnp.tril(jnp.ones((SGU_CHUNK, SGU_CHUNK), F32))
    wsp_p = (w_s[layer] * tril).astype(BF16)
    bsp_p = jnp.repeat(b_s[layer].T, SGU_GROUP_DIM, axis=1)
    wsp_s = jnp.repeat(w_s[layer][:, 0, 0], SGU_GROUP_DIM).reshape(1, -1)
    bsp_s = jnp.repeat(b_s[layer][:, 0], SGU_GROUP_DIM).reshape(1, -1)

    cos_p, sin_p = _rope_tables(jnp.arange(seq))
    cos_s, sin_s = _rope_tables(jnp.repeat(past + jnp.arange(dl), db))

    proj = functools.partial(_proj, g_mix=row(g_mix), w_in_b=w_in_b, ln_g=row(ln_g), ln_b=row(ln_b),
                             w_br_sgu_b=w_br_sgu_b)
    q_p, k_p, v_p, kb_p, vb_p, kmean_p, sga_p, msgu_p = proj(
        x_prompt.reshape(tp, d), cos_p, sin_p, wsp=wsp_p, bsp=bsp_p, chunked=True, seq=seq)
    q_s, k_s, v_s, vn_s, sga_s, msgu_s = proj(
        x_sample.reshape(ts, d), cos_s, sin_s, wsp=wsp_s, bsp=bsp_s, chunked=False, seq=ts)

    a_p, pagesums = _moba_prompt(q_p.reshape(bsz, seq, d), kb_p.reshape(bsz, seq, d), vb_p.reshape(bsz, seq, d),
                                 kmean_p.reshape(bsz, seq // MOBA_BLOCK, d), cache_k[layer], page_table)

    heads = lambda a: a.reshape(db, N_HEADS, HEAD_DIM)
    sel = _blocksel(heads(q_s), pagesums.reshape(db, n_pages, N_HEADS, HEAD_DIM), n_sel_s, ppb)
    logical = sel.transpose(0, 2, 1)[..., None] * ppb + jnp.arange(ppb)
    hit = logical.reshape(db, N_HEADS, -1, 1) == jnp.arange(n_pages)
    phys = jnp.sum(jnp.where(hit, page_table[:, None, None, :], 0), axis=-1)
    a_s = _decode(phys.astype(jnp.int32), heads(q_s), heads(k_s), heads(v_s), cache_k[layer], cache_v[layer])

    post = functools.partial(_post, w_br_attn_b=w_br_attn_b, w_out_b=w_out_b, g_ffn=row(g_ffn),
                             w_router_t=w_router[layer].T, b_router_col=b_router[layer].reshape(-1, 1))
    x1_p, h2_p, idx_p, gate_p, rank_p, cnt_p = post(
        a_p.reshape(tp, d), sga_p, msgu_p, x_prompt.reshape(tp, d), jnp.zeros((N_EXPERTS, HEAD_DIM), F32))
    x1_s, h2_s, idx_s, gate_s, rank_s, cnt = post(a_s.reshape(ts, d), sga_s, msgu_s, x_sample.reshape(ts, d), cnt_p)

    n_blk = -(-((tp + ts) * TOP_K + N_EXPERTS * (MOE_BLOCK - 1)) // MOE_BLOCK)
    pstart, blk_exp, blk_rows, last_blk, tail_blk = _route(cnt[:, 0].astype(jnp.int32), n_blk)
    dest_p = _dest_rows(idx_p, rank_p, pstart)
    dest_s = _dest_rows(idx_s, rank_s, pstart)
    xs = _dispatch(h2_p, h2_s, dest_p, dest_s, last_blk, tail_blk, n_blk * MOE_BLOCK)
    eo = _experts(xs, blk_exp, blk_rows, w_gu[layer], b_gu[layer][:, None, :], w_dn[layer], b_dn[layer][:, None, :])
    y_p = _final(x1_p, eo, dest_p, gate_p.T, g_final.reshape(1, -1))
    y_s = _final(x1_s, eo, dest_s, gate_s.T, g_final.reshape(1, -1))

    return (y_p.reshape(bsz, seq, d), y_s.reshape(db, dl, d),
            k_p.reshape(depth, bsz, seq, N_HEADS, HEAD_DIM), v_p.reshape(depth, bsz, seq, N_HEADS, HEAD_DIM),
            k_s.reshape(depth, db, dl, N_HEADS, HEAD_DIM), v_s.reshape(depth, db, dl, N_HEADS, HEAD_DIM),
            vn_s.reshape(depth, db, dl, d))
```
